```python
import jax, jax.numpy as jnp
from jax import lax
import numpy as np

D_MODEL = 1024
BATCH = 4
SEQ = 4096
DEPTH = 2

N_BRANCH = 4
BRANCH_WIDTH = 256
SGU_HEADS = 4
SGU_HEAD_DIM = BRANCH_WIDTH // SGU_HEADS
SGU_CHUNK = 128
CONV_WIDTH = 3
POOL_WINDOWS = (2, 4, 8, 16)
POOL_GROUPS = len(POOL_WINDOWS)
POOL_GROUP_DIM = BRANCH_WIDTH // POOL_GROUPS
GLA_HEADS = 4
GLA_DK = 32
GLA_DV = BRANCH_WIDTH // GLA_HEADS
GLA_RANK = 16
GLA_TAU = 16.0
GLA_CHUNK = 64
D_FF = 2816
N_EXPERTS = 8
TOP_K = 2
EPS = 1e-6

A_COLS = 2 * BRANCH_WIDTH
B_COLS = 3 * BRANCH_WIDTH
C_COLS = BRANCH_WIDTH
D_COLS = 2 * GLA_HEADS * GLA_DK + 2 * GLA_HEADS * GLA_DV + GLA_RANK
G_COLS = N_BRANCH * D_MODEL
IN_COLS = A_COLS + B_COLS + C_COLS + D_COLS + G_COLS
SPLITS = (A_COLS, A_COLS + B_COLS, A_COLS + B_COLS + C_COLS, A_COLS + B_COLS + C_COLS + D_COLS)
N_DENSE = (DEPTH + 1) // 2
N_MOE = DEPTH // 2

kernel_name = "hybrid_gated_multimixer_moe_trunk"


def rmsnorm(x, g):
    x32 = x.astype(jnp.float32)
    y = x32 * lax.rsqrt(jnp.mean(x32 * x32, axis=-1, keepdims=True) + EPS)
    return (y * g.astype(jnp.float32)).astype(x.dtype)


def sgu_mixer(za, sgu_w, sgu_b, sgu_norm):
    z = jax.nn.gelu(za)
    u, v = jnp.split(z, 2, axis=-1)
    b, s, _ = v.shape
    nc = s // SGU_CHUNK
    v = rmsnorm(v.reshape(b, s, SGU_HEADS, SGU_HEAD_DIM), sgu_norm.reshape(SGU_HEADS, SGU_HEAD_DIM))
    v = v.reshape(b, nc, SGU_CHUNK, SGU_HEADS, SGU_HEAD_DIM)
    causal = jnp.tril(jnp.ones((SGU_CHUNK, SGU_CHUNK), dtype=bool))
    w = jnp.where(causal[None], sgu_w, 0)
    mixed = jnp.einsum('hts,bnshd->bnthd', w, v) + sgu_b.T[None, None, :, :, None]
    return u * mixed.reshape(b, s, BRANCH_WIDTH)


def shortconv_mixer(zb, conv_w, conv_b):
    xin, gate_b, gate_c = jnp.split(zb, 3, axis=-1)
    y = gate_c * xin
    s = y.shape[1]
    yp = jnp.pad(y, ((0, 0), (CONV_WIDTH - 1, 0), (0, 0)))
    conv = conv_b + yp[:, 0:s] * conv_w[0]
    for i in range(1, CONV_WIDTH):
        conv = conv + yp[:, i:i + s] * conv_w[i]
    return gate_b * conv


def pool_mixer(zc, pool_w, pool_scale):
    b, s, _ = zc.shape
    z32 = zc.astype(jnp.float32)
    incl = jnp.cumsum(z32, axis=1)
    t = jnp.arange(s)
    outs = []
    for g, w in enumerate(POOL_WINDOWS):
        sl = slice(g * POOL_GROUP_DIM, (g + 1) * POOL_GROUP_DIM)
        c = incl[..., sl]
        lag = jnp.pad(c[:, :s - w], ((0, 0), (w, 0), (0, 0)))
        count = jnp.minimum(t + 1, w).astype(jnp.float32)[None, :, None]
        outs.append((c - lag) / count - z32[..., sl])
    pooled = jnp.stack(outs, axis=2).astype(zc.dtype)
    mixed = jnp.einsum('bsgc,gcd->bsgd', pooled, pool_w)
    return mixed.reshape(b, s, BRANCH_WIDTH) * pool_scale


def gla_mixer(zd, gla_wg2, gla_bg, gla_norm):
    b, s, _ = zd.shape
    hk = GLA_HEADS * GLA_DK
    hv = GLA_HEADS * GLA_DV
    q, k, v, r, glr = jnp.split(zd, [hk, 2 * hk, 2 * hk + hv, 2 * hk + 2 * hv], axis=-1)
    log_g = jax.nn.log_sigmoid((glr @ gla_wg2 + gla_bg).astype(jnp.float32)) / GLA_TAU
    nc = s // GLA_CHUNK

    def chunked(a, d):
        return a.reshape(b, nc, GLA_CHUNK, GLA_HEADS, d).astype(jnp.float32)

    q = chunked(q, GLA_DK) * (GLA_DK ** -0.5)
    k = chunked(k, GLA_DK)
    v = chunked(v, GLA_DV)
    cum = jnp.cumsum(chunked(log_g, GLA_DK), axis=2)
    last = cum[:, :, -1]
    q_dec = q * jnp.exp(cum)
    k_inv = k * jnp.exp(-cum)
    k_end = k * jnp.exp(last[:, :, None] - cum)
    causal = jnp.tril(jnp.ones((GLA_CHUNK, GLA_CHUNK), dtype=bool))
    scores = jnp.where(causal, jnp.einsum('bnthk,bnshk->bnhts', q_dec, k_inv), 0.0)
    o_intra = jnp.einsum('bnhts,bnshv->bnthv', scores, v)
    kv = jnp.einsum('bnshk,bnshv->bnhkv', k_end, v)
    decay = jnp.exp(last)

    def step(state, inp):
        kv_n, dec_n = inp
        return dec_n[..., None] * state + kv_n, state

    init = jnp.zeros((b, GLA_HEADS, GLA_DK, GLA_DV), jnp.float32)
    _, states = lax.scan(step, init, (jnp.moveaxis(kv, 1, 0), jnp.moveaxis(decay, 1, 0)))
    states = jnp.moveaxis(states, 0, 1)
    o_inter = jnp.einsum('bnthk,bnhkv->bnthv', q_dec, states)
    o = rmsnorm((o_intra + o_inter).reshape(b, s, GLA_HEADS, GLA_DV), gla_norm)
    return jax.nn.silu(r) * o.reshape(b, s, hv).astype(zd.dtype)


def mixer_block(h, w_in, b_gate, sgu_w, sgu_b, sgu_norm, conv_w, conv_b, pool_w, pool_scale,
                gla_wg2, gla_bg, gla_norm, branch_proj, w_out):
    b, s, _ = h.shape
    z = h @ w_in
    za, zb, zc, zd, zg = jnp.split(z, SPLITS, axis=-1)
    ya = sgu_mixer(za, sgu_w, sgu_b, sgu_norm)
    yb = shortconv_mixer(zb, conv_w, conv_b)
    yc = pool_mixer(zc, pool_w, pool_scale)
    yd = gla_mixer(zd, gla_wg2, gla_bg, gla_norm)
    branches = jnp.stack([ya, yb, yc, yd], axis=2)
    proj = jnp.einsum('bsiw,iwd->bsid', branches, branch_proj)
    gates = jax.nn.sigmoid(zg.reshape(b, s, N_BRANCH, D_MODEL) + b_gate)
    merged = jnp.sum(gates * proj, axis=2)
    return merged @ w_out


def swiglu(t, w1, w3, w2):
    return (jax.nn.silu(t @ w1) * (t @ w3)) @ w2


def moe_swiglu(h, router, w1, w3, w2):
    b, s, d = h.shape
    t = h.reshape(b * s, d)
    logits = (t @ router).astype(jnp.float32)
    top_vals, top_idx = lax.top_k(logits, TOP_K)
    weights = jax.nn.softmax(top_vals, axis=-1)
    combine = jnp.sum(jax.nn.one_hot(top_idx, N_EXPERTS, dtype=jnp.float32) * weights[..., None], axis=1)
    combine = combine.astype(t.dtype)
    out = combine[:, 0:1] * swiglu(t, w1[0], w3[0], w2[0])
    for e in range(1, N_EXPERTS):
        out = out + combine[:, e:e + 1] * swiglu(t, w1[e], w3[e], w2[e])
    return out.reshape(b, s, d)


def setup_inputs(seed: int = 0) -> dict:
    key = jax.random.key(seed)
    ks = jax.random.split(key, 25)
    L = DEPTH
    f32 = jnp.float32

    def nrm(k, shape, scale):
        return jax.random.normal(k, shape, f32) * scale

    def gain(k, shape):
        return 1.0 + 0.01 * jax.random.normal(k, shape, f32)

    return {
        "x": nrm(ks[0], (BATCH, SEQ, D_MODEL), 1.0),
        "norm_mix": gain(ks[1], (L, D_MODEL)),
        "w_in": nrm(ks[2], (L, D_MODEL, IN_COLS), D_MODEL ** -0.5),
        "b_gate": nrm(ks[3], (L, N_BRANCH, D_MODEL), 0.01),
        "sgu_w": nrm(ks[4], (L, SGU_HEADS, SGU_CHUNK, SGU_CHUNK), SGU_CHUNK ** -0.5),
        "sgu_b": gain(ks[5], (L, SGU_HEADS, SGU_CHUNK)),
        "sgu_norm": gain(ks[6], (L, BRANCH_WIDTH)),
        "conv_w": nrm(ks[7], (L, CONV_WIDTH, BRANCH_WIDTH), CONV_WIDTH ** -0.5),
        "conv_b": nrm(ks[8], (L, BRANCH_WIDTH), 0.01),
        "pool_w": nrm(ks[9], (L, POOL_GROUPS, POOL_GROUP_DIM, POOL_GROUP_DIM), POOL_GROUP_DIM ** -0.5),
        "pool_scale": gain(ks[10], (L, BRANCH_WIDTH)),
        "gla_wg2": nrm(ks[11], (L, GLA_RANK, GLA_HEADS * GLA_DK), GLA_RANK ** -0.5),
        "gla_bg": nrm(ks[12], (L, GLA_HEADS * GLA_DK), 0.01),
        "gla_norm": gain(ks[13], (L, GLA_DV)),
        "branch_proj": nrm(ks[14], (L, N_BRANCH, BRANCH_WIDTH, D_MODEL), BRANCH_WIDTH ** -0.5),
        "w_out": nrm(ks[15], (L, D_MODEL, D_MODEL), D_MODEL ** -0.5),
        "norm_ffn": gain(ks[16], (L, D_MODEL)),
        "ffn_w1": nrm(ks[17], (N_DENSE, D_MODEL, D_FF), D_MODEL ** -0.5),
        "ffn_w3": nrm(ks[18], (N_DENSE, D_MODEL, D_FF), D_MODEL ** -0.5),
        "ffn_w2": nrm(ks[19], (N_DENSE, D_FF, D_MODEL), D_FF ** -0.5),
        "moe_router": nrm(ks[20], (N_MOE, D_MODEL, N_EXPERTS), D_MODEL ** -0.5),
        "moe_w1": nrm(ks[21], (N_MOE, N_EXPERTS, D_MODEL, D_FF), D_MODEL ** -0.5),
        "moe_w3": nrm(ks[22], (N_MOE, N_EXPERTS, D_MODEL, D_FF), D_MODEL ** -0.5),
        "moe_w2": nrm(ks[23], (N_MOE, N_EXPERTS, D_FF, D_MODEL), D_FF ** -0.5),
        "final_norm": gain(ks[24], (D_MODEL,)),
    }


def reference(x, norm_mix, w_in, b_gate, sgu_w, sgu_b, sgu_norm, conv_w, conv_b, pool_w, pool_scale,
              gla_wg2, gla_bg, gla_norm, branch_proj, w_out, norm_ffn, ffn_w1, ffn_w3, ffn_w2,
              moe_router, moe_w1, moe_w3, moe_w2, final_norm):
    for l in range(DEPTH):
        h = rmsnorm(x, norm_mix[l])
        x = x + mixer_block(h, w_in[l], b_gate[l], sgu_w[l], sgu_b[l], sgu_norm[l], conv_w[l], conv_b[l],
                            pool_w[l], pool_scale[l], gla_wg2[l], gla_bg[l], gla_norm[l],
                            branch_proj[l], w_out[l])
        h = rmsnorm(x, norm_ffn[l])
        if l % 2 == 0:
            i = l // 2
            x = x + swiglu(h, ffn_w1[i], ffn_w3[i], ffn_w2[i])
        else:
            i = l // 2
            x = x + moe_swiglu(h, moe_router[i], moe_w1[i], moe_w3[i], moe_w2[i])
    return rmsnorm(x, final_norm)
```

```python
import functools

import numpy as np
import jax
import jax.numpy as jnp
from jax import lax
from jax.experimental import pallas as pl
from jax.experimental.pallas import tpu as pltpu

F32 = jnp.float32
BF16 = jnp.bfloat16

D_MODEL = 1024
N_BRANCH = 4
BRANCH_WIDTH = 256
SGU_HEADS = 4
SGU_HEAD_DIM = BRANCH_WIDTH // SGU_HEADS
SGU_CHUNK = 128
CONV_WIDTH = 3
POOL_WINDOWS = (2, 4, 8, 16)
POOL_GROUP_DIM = BRANCH_WIDTH // len(POOL_WINDOWS)
POOL_HISTORY = 16
CONV_HISTORY = 8
GLA_HEADS = 4
GLA_DK = 32
GLA_DV = BRANCH_WIDTH // GLA_HEADS
GLA_HK = GLA_HEADS * GLA_DK
GLA_RANK = 16
GLA_TAU = 16.0
GLA_CHUNK = 64
D_FF = 2816
N_EXPERTS = 8
TOP_K = 2
EPS = 1e-6

A_COLS = 2 * BRANCH_WIDTH
B_COLS = 3 * BRANCH_WIDTH
C_COLS = BRANCH_WIDTH
QKVR_COLS = 2 * GLA_HK + 2 * BRANCH_WIDTH
LANES = 128
OFF_A = 0
OFF_B = OFF_A + A_COLS
OFF_C = OFF_B + B_COLS
OFF_D = OFF_C + C_COLS
OFF_GLR = OFF_D + QKVR_COLS
MIX_COLS = OFF_GLR + LANES
GATE_OFF = OFF_GLR + GLA_RANK

MIX_TS = 512
ROW_TILE = 512
GROUP_TILE = 512
FF_CHUNK = 256
VMEM_LIMIT = 56 * 1024 * 1024


def _dot(a, b):
    return jnp.dot(a, b, preferred_element_type=F32)


def _dot_nt(a, b):
    return lax.dot_general(a, b, (((1,), (1,)), ((), ())), preferred_element_type=F32)


def _dot_tn(a, b):
    return lax.dot_general(a, b, (((0,), (0,)), ((), ())), preferred_element_type=F32)


def _split(x):
    hi = x.astype(BF16)
    lo = (x - hi.astype(F32)).astype(BF16)
    return hi, lo


def _rms(x, g):
    ms = jnp.mean(x * x, axis=-1, keepdims=True)
    return x * lax.rsqrt(ms + EPS) * g


def _sigmoid(x):
    return 1.0 / (1.0 + jnp.exp(-x))


def _silu(x):
    return x * _sigmoid(x)


def _gelu_tanh(x):
    c = np.float32(np.sqrt(2.0 / np.pi))
    return x * (0.5 * (1.0 + jnp.tanh(c * (x + 0.044715 * (x * x * x)))))


def _group_mean_matrix(n, group):
    r = lax.broadcasted_iota(jnp.int32, (n, n), 0) // group
    c = lax.broadcasted_iota(jnp.int32, (n, n), 1) // group
    return jnp.where(r == c, 1.0 / group, 0.0).astype(BF16)


def _group_rms(x, bd, g):
    hi, lo = _split(x * x)
    ms = _dot(hi, bd) + _dot(lo, bd)
    return x * lax.rsqrt(ms + EPS) * g


def _mixer_kernel(x_ref, g_ref, w_ref, sguw_ref, sgub_ref, sgun_ref, cw_ref, cb_ref, pw_ref, ps_ref,
                  wg2_ref, bg_ref, gn_ref, y_ref,
                  state_ref, ybuf_ref, zbuf_ref, qd_ref, ki_ref, ke_ref, v_ref, el_ref, o_ref):
    si = pl.program_id(1)
    ts = x_ref.shape[1]
    x = x_ref[0]
    h = _rms(x, g_ref[...]).astype(BF16)

    @pl.when(si == 0)
    def _():
        state_ref[...] = jnp.zeros_like(state_ref)
        ybuf_ref[0:CONV_HISTORY, :] = jnp.zeros((CONV_HISTORY, BRANCH_WIDTH), F32)
        zbuf_ref[0:POOL_HISTORY, :] = jnp.zeros((POOL_HISTORY, BRANCH_WIDTH), F32)

    bd64 = _group_mean_matrix(BRANCH_WIDTH, SGU_HEAD_DIM)
    lane = lax.broadcasted_iota(jnp.int32, (1, BRANCH_WIDTH), 1)

    z = _gelu_tanh(_dot(h, w_ref[:, OFF_A:OFF_A + A_COLS]))
    u = z[:, :BRANCH_WIDTH]
    vn = _group_rms(z[:, BRANCH_WIDTH:], bd64, sgun_ref[...])
    lane_head = lane // SGU_HEAD_DIM
    for c in range(ts // SGU_CHUNK):
        rows = slice(c * SGU_CHUNK, (c + 1) * SGU_CHUNK)
        vc = vn[rows]
        vst = jnp.concatenate([jnp.where(lane_head == hh, vc, 0.0) for hh in range(SGU_HEADS)], axis=0)
        mixed = _dot(sguw_ref[...], vst.astype(BF16)) + sgub_ref[...]
        y_ref[0, rows, 0:BRANCH_WIDTH] = (u[rows] * mixed).astype(BF16)

    zb = _dot(h, w_ref[:, OFF_B:OFF_B + B_COLS])
    gate_b = zb[:, BRANCH_WIDTH:2 * BRANCH_WIDTH]
    yv = zb[:, 2 * BRANCH_WIDTH:] * zb[:, :BRANCH_WIDTH]
    ybuf_ref[CONV_HISTORY:CONV_HISTORY + ts, :] = yv
    conv = cb_ref[...] + yv * cw_ref[CONV_WIDTH - 1:CONV_WIDTH, :]
    for i in range(CONV_WIDTH - 1):
        back = CONV_WIDTH - 1 - i
        conv = conv + ybuf_ref[CONV_HISTORY - back:CONV_HISTORY - back + ts, :] * cw_ref[i:i + 1, :]
    ybuf_ref[0:CONV_HISTORY, :] = ybuf_ref[ts:ts + CONV_HISTORY, :]
    y_ref[0, :, BRANCH_WIDTH:2 * BRANCH_WIDTH] = (gate_b * conv).astype(BF16)

    zc = _dot(h, w_ref[:, OFF_C:OFF_C + C_COLS])
    zbuf_ref[POOL_HISTORY:POOL_HISTORY + ts, :] = zc
    run = zc
    sums = {}
    for j in range(1, POOL_WINDOWS[-1]):
        run = run + zbuf_ref[POOL_HISTORY - j:POOL_HISTORY - j + ts, :]
        if j + 1 in POOL_WINDOWS:
            sums[j + 1] = run
    zbuf_ref[0:POOL_HISTORY, :] = zbuf_ref[ts:ts + POOL_HISTORY, :]
    lane_group = lane // POOL_GROUP_DIM
    wsum = sums[POOL_WINDOWS[-1]]
    win = jnp.full((1, BRANCH_WIDTH), POOL_WINDOWS[-1], jnp.int32)
    for gi in range(len(POOL_WINDOWS) - 2, -1, -1):
        wsum = jnp.where(lane_group == gi, sums[POOL_WINDOWS[gi]], wsum)
        win = jnp.where(lane_group == gi, POOL_WINDOWS[gi], win)
    tpos = si * ts + lax.broadcasted_iota(jnp.int32, (ts, 1), 0)
    count = jnp.minimum(tpos + 1, win).astype(F32)
    pooled = wsum / count - zc
    y_ref[0, :, 2 * BRANCH_WIDTH:3 * BRANCH_WIDTH] = (
        _dot(pooled.astype(BF16), pw_ref[...]) * ps_ref[...]).astype(BF16)

    zd = _dot(h, w_ref[:, OFF_D:OFF_D + QKVR_COLS + LANES])
    q = zd[:, 0:GLA_HK]
    k = zd[:, GLA_HK:2 * GLA_HK]
    v = zd[:, 2 * GLA_HK:2 * GLA_HK + BRANCH_WIDTH]
    r = zd[:, 2 * GLA_HK + BRANCH_WIDTH:QKVR_COLS]
    glr = zd[:, QKVR_COLS:]
    a = _dot(glr.astype(BF16), wg2_ref[...]) + bg_ref[...]
    log_g = (jnp.minimum(a, 0.0) - jnp.log(1.0 + jnp.exp(-jnp.abs(a)))) * (1.0 / GLA_TAU)
    blk = 4 * GLA_CHUNK
    rr = lax.broadcasted_iota(jnp.int32, (blk, blk), 0)
    cc = lax.broadcasted_iota(jnp.int32, (blk, blk), 1)
    same = (rr // GLA_CHUNK) == (cc // GLA_CHUNK)
    tri = jnp.where(same & (cc <= rr), 1.0, 0.0).astype(BF16)
    ones = jnp.where(same, 1.0, 0.0).astype(BF16)
    cums, lasts = [], []
    for b in range(ts // blk):
        hi, lo = _split(log_g[b * blk:(b + 1) * blk])
        cums.append(_dot(tri, hi) + _dot(tri, lo))
        lasts.append(_dot(ones, hi) + _dot(ones, lo))
    cum = jnp.concatenate(cums, axis=0)
    last = jnp.concatenate(lasts, axis=0)
    qd_ref[...] = (q * (GLA_DK ** -0.5) * jnp.exp(cum)).astype(BF16)
    ki_ref[...] = (k * jnp.exp(-cum)).astype(BF16)
    ke_ref[...] = (k * jnp.exp(last - cum)).astype(BF16)
    v_ref[...] = v.astype(BF16)
    el_ref[...] = jnp.exp(last)

    nrow = GLA_HEADS * GLA_CHUNK
    row_head = lax.broadcasted_iota(jnp.int32, (nrow, 1), 0) // GLA_CHUNK
    mask_k = jnp.where(row_head == lax.broadcasted_iota(jnp.int32, (nrow, GLA_HK), 1) // GLA_DK,
                       1.0, 0.0).astype(BF16)
    mask_v = jnp.where(row_head == lax.broadcasted_iota(jnp.int32, (nrow, BRANCH_WIDTH), 1) // GLA_DV,
                       1.0, 0.0).astype(BF16)
    causal = (lax.broadcasted_iota(jnp.int32, (GLA_CHUNK, nrow), 1) % GLA_CHUNK
              <= lax.broadcasted_iota(jnp.int32, (GLA_CHUNK, nrow), 0))
    mask_s = (lax.broadcasted_iota(jnp.int32, (BRANCH_WIDTH, GLA_HK), 0) // GLA_DV
              == lax.broadcasted_iota(jnp.int32, (BRANCH_WIDTH, GLA_HK), 1) // GLA_DK)

    def chunk_step(n, carry):
        rows = pl.ds(pl.multiple_of(n * GLA_CHUNK, GLA_CHUNK), GLA_CHUNK)
        qd = qd_ref[rows, :]
        ki = ki_ref[rows, :]
        ke = ke_ref[rows, :]
        vc = v_ref[rows, :]
        kbd = jnp.concatenate([ki] * GLA_HEADS, axis=0) * mask_k
        vbd = jnp.concatenate([vc] * GLA_HEADS, axis=0) * mask_v
        scores = jnp.where(causal, _dot_nt(qd, kbd), 0.0)
        state = state_ref[...]
        o_ref[rows, :] = _dot(scores.astype(BF16), vbd) + _dot_nt(qd, state.astype(BF16))
        decay = el_ref[pl.ds(pl.multiple_of(n * GLA_CHUNK, GLA_CHUNK), 1), :]
        kv = _dot_tn(vc, ke)
        state_ref[...] = state * decay + jnp.where(mask_s, kv, 0.0)
        return carry

    lax.fori_loop(0, ts // GLA_CHUNK, chunk_step, 0)
    on = _group_rms(o_ref[...], bd64, gn_ref[...])
    y_ref[0, :, 3 * BRANCH_WIDTH:] = (_silu(r) * on).astype(BF16)


def _mixer_call(x, g, w_mix, sguw, sgub, sgun, cw, cb, pw, ps, wg2, bg, gn):
    b, s, d = x.shape
    ts = MIX_TS

    def const(arr):
        return pl.BlockSpec(arr.shape, lambda bi, si: (0,) * arr.ndim)

    consts = (g, w_mix, sguw, sgub, sgun, cw, cb, pw, ps, wg2, bg, gn)
    return pl.pallas_call(
        _mixer_kernel,
        grid=(b, s // ts),
        in_specs=[pl.BlockSpec((1, ts, d), lambda bi, si: (bi, si, 0))] + [const(c) for c in consts],
        out_specs=pl.BlockSpec((1, ts, N_BRANCH * BRANCH_WIDTH), lambda bi, si: (bi, si, 0)),
        out_shape=jax.ShapeDtypeStruct((b, s, N_BRANCH * BRANCH_WIDTH), BF16),
        scratch_shapes=[
            pltpu.VMEM((BRANCH_WIDTH, GLA_HK), F32),
            pltpu.VMEM((CONV_HISTORY + ts, BRANCH_WIDTH), F32),
            pltpu.VMEM((POOL_HISTORY + ts, BRANCH_WIDTH), F32),
            pltpu.VMEM((ts, GLA_HK), BF16),
            pltpu.VMEM((ts, GLA_HK), BF16),
            pltpu.VMEM((ts, GLA_HK), BF16),
            pltpu.VMEM((ts, BRANCH_WIDTH), BF16),
            pltpu.VMEM((ts, GLA_HK), F32),
            pltpu.VMEM((ts, BRANCH_WIDTH), F32),
        ],
        compiler_params=pltpu.CompilerParams(
            dimension_semantics=("arbitrary", "arbitrary"), vmem_limit_bytes=VMEM_LIMIT),
        name="mixer",
    )(x, *consts)


def _merge_kernel(x_ref, y_ref, g_ref, wg_ref, bgate_ref, p_ref, wo_ref, o_ref):
    x = x_ref[...]
    h = _rms(x, g_ref[...]).astype(BF16)
    merged = None
    for i in range(N_BRANCH):
        cols = slice(i * D_MODEL, (i + 1) * D_MODEL)
        gate = _sigmoid(_dot(h, wg_ref[:, cols]) + bgate_ref[:, cols])
        rows = slice(i * BRANCH_WIDTH, (i + 1) * BRANCH_WIDTH)
        term = gate * _dot(y_ref[:, rows], p_ref[rows, :])
        merged = term if merged is None else merged + term
    o_ref[...] = x + _dot(merged.astype(BF16), wo_ref[...])


def _row_spec(tm, width):
    return pl.BlockSpec((tm, width), lambda i: (i, 0))


def _const_spec(arr):
    return pl.BlockSpec(arr.shape, lambda i: (0,) * arr.ndim)


def _merge_call(x2, y2, g, wg, bgate, p, wo):
    t, d = x2.shape
    tm = ROW_TILE
    consts = (g, wg, bgate, p, wo)
    return pl.pallas_call(
        _merge_kernel,
        grid=(t // tm,),
        in_specs=[_row_spec(tm, d), _row_spec(tm, y2.shape[1])] + [_const_spec(c) for c in consts],
        out_specs=_row_spec(tm, d),
        out_shape=jax.ShapeDtypeStruct((t, d), F32),
        compiler_params=pltpu.CompilerParams(
            dimension_semantics=("arbitrary",), vmem_limit_bytes=VMEM_LIMIT),
        name="merge",
    )(x2, y2, *consts)


def _swiglu_rows(hb, w1_ref, w3_ref, w2_ref, lead):
    acc = None
    for f in range(D_FF // FF_CHUNK):
        cols = slice(f * FF_CHUNK, (f + 1) * FF_CHUNK)
        up = _dot(hb, w1_ref[lead + (slice(None), cols)])
        gt = _dot(hb, w3_ref[lead + (slice(None), cols)])
        part = _dot((_silu(up) * gt).astype(BF16), w2_ref[lead + (cols, slice(None))])
        acc = part if acc is None else acc + part
    return acc


def _ffn_kernel(x_ref, g_ref, w1_ref, w3_ref, w2_ref, o_ref):
    x = x_ref[...]
    hb = _rms(x, g_ref[...]).astype(BF16)
    o_ref[...] = x + _swiglu_rows(hb, w1_ref, w3_ref, w2_ref, ())


def _ffn_call(x2, g, w1, w3, w2):
    t, d = x2.shape
    tm = ROW_TILE
    consts = (g, w1, w3, w2)
    return pl.pallas_call(
        _ffn_kernel,
        grid=(t // tm,),
        in_specs=[_row_spec(tm, d)] + [_const_spec(c) for c in consts],
        out_specs=_row_spec(tm, d),
        out_shape=jax.ShapeDtypeStruct((t, d), F32),
        compiler_params=pltpu.CompilerParams(
            dimension_semantics=("arbitrary",), vmem_limit_bytes=VMEM_LIMIT),
        name="ffn",
    )(x2, *consts)


ROUTE_IDX0 = N_EXPERTS
ROUTE_W0 = N_EXPERTS + TOP_K


def _router_kernel(x_ref, g_ref, rhi_ref, rlo_ref, h_ref, route_ref):
    x = x_ref[...]
    h = _rms(x, g_ref[...])
    h_ref[...] = h
    hi, lo = _split(h)
    logits = _dot(hi, rhi_ref[...]) + _dot(hi, rlo_ref[...]) + _dot(lo, rhi_ref[...])
    lane = lax.broadcasted_iota(jnp.int32, logits.shape, 1)
    neg = jnp.float32(-jnp.inf)
    logits = jnp.where(lane < N_EXPERTS, logits, neg)
    m1 = jnp.max(logits, axis=-1, keepdims=True)
    i1 = jnp.min(jnp.where(logits == m1, lane, LANES), axis=-1, keepdims=True)
    rest = jnp.where(lane == i1, neg, logits)
    m2 = jnp.max(rest, axis=-1, keepdims=True)
    i2 = jnp.min(jnp.where(rest == m2, lane, LANES), axis=-1, keepdims=True)
    e2 = jnp.exp(m2 - m1)
    w1 = 1.0 / (1.0 + e2)
    w2 = e2 / (1.0 + e2)
    out = jnp.where(lane == ROUTE_IDX0, i1.astype(F32), 0.0)
    out = jnp.where(lane == ROUTE_IDX0 + 1, i2.astype(F32), out)
    out = jnp.where(lane == ROUTE_W0, w1, out)
    out = jnp.where(lane == ROUTE_W0 + 1, w2, out)
    route_ref[...] = out


def _router_call(x2, g, rhi, rlo):
    t, d = x2.shape
    tm = ROW_TILE
    consts = (g, rhi, rlo)
    return pl.pallas_call(
        _router_kernel,
        grid=(t // tm,),
        in_specs=[_row_spec(tm, d)] + [_const_spec(c) for c in consts],
        out_specs=[_row_spec(tm, d), _row_spec(tm, LANES)],
        out_shape=[jax.ShapeDtypeStruct((t, d), F32), jax.ShapeDtypeStruct((t, LANES), F32)],
        compiler_params=pltpu.CompilerParams(
            dimension_semantics=("arbitrary",), vmem_limit_bytes=VMEM_LIMIT),
        name="router",
    )(x2, *consts)


def _row_copy(src_hbm, row, dst, slot, sem):
    return pltpu.make_async_copy(src_hbm.at[pl.ds(row, 1), :], dst.at[pl.ds(slot, 1), :], sem)


def _gather_rows(idx_ref, src_hbm, dst, sem, n):
    def start(r, c):
        _row_copy(src_hbm, idx_ref[0, 0, r], dst, r, sem).start()
        return c

    lax.fori_loop(0, n, start, 0)

    def wait(r, c):
        _row_copy(src_hbm, 0, dst, r, sem).wait()
        return c

    lax.fori_loop(0, n, wait, 0)


def _group_kernel(texp_ref, tvalid_ref, rows_ref, h_hbm, w1_ref, w3_ref, w2_ref, ys_ref, xbuf, sem):
    i = pl.program_id(0)

    @pl.when(tvalid_ref[i] == 1)
    def _():
        _gather_rows(rows_ref, h_hbm, xbuf, sem, xbuf.shape[0])
        ys_ref[...] = _swiglu_rows(xbuf[...].astype(BF16), w1_ref, w3_ref, w2_ref, (0,))

    @pl.when(tvalid_ref[i] == 0)
    def _():
        ys_ref[...] = jnp.zeros_like(ys_ref)


def _group_call(tile_expert, tile_valid, row_token, h2, w1, w3, w2):
    n_tiles = tile_expert.shape[0]
    tm = GROUP_TILE
    d = h2.shape[1]

    def wspec(arr):
        return pl.BlockSpec((1,) + arr.shape[1:], lambda i, te, tv: (te[i], 0, 0))

    grid_spec = pltpu.PrefetchScalarGridSpec(
        num_scalar_prefetch=2,
        grid=(n_tiles,),
        in_specs=[
            pl.BlockSpec((1, 1, tm), lambda i, te, tv: (i, 0, 0), memory_space=pltpu.SMEM),
            pl.BlockSpec(memory_space=pl.ANY),
            wspec(w1), wspec(w3), wspec(w2),
        ],
        out_specs=pl.BlockSpec((tm, d), lambda i, te, tv: (i, 0)),
        scratch_shapes=[pltpu.VMEM((tm, d), F32), pltpu.SemaphoreType.DMA(())],
    )
    return pl.pallas_call(
        _group_kernel,
        grid_spec=grid_spec,
        out_shape=jax.ShapeDtypeStruct((n_tiles * tm, d), F32),
        compiler_params=pltpu.CompilerParams(
            dimension_semantics=("arbitrary",), vmem_limit_bytes=VMEM_LIMIT),
        name="moe_group",
    )(tile_expert, tile_valid, row_token.reshape(n_tiles, 1, tm), h2, w1, w3, w2)


def _combine_kernel(p0_ref, p1_ref, ys_hbm, x_ref, route_ref, g_ref, o_ref, buf0, buf1, sem0, sem1, *,
                    apply_final_norm):
    tm = x_ref.shape[0]

    def start(r, c):
        _row_copy(ys_hbm, p0_ref[0, 0, r], buf0, r, sem0).start()
        _row_copy(ys_hbm, p1_ref[0, 0, r], buf1, r, sem1).start()
        return c

    lax.fori_loop(0, tm, start, 0)

    def wait(r, c):
        _row_copy(ys_hbm, 0, buf0, r, sem0).wait()
        _row_copy(ys_hbm, 0, buf1, r, sem1).wait()
        return c

    lax.fori_loop(0, tm, wait, 0)
    route = route_ref[...]
    w0 = route[:, ROUTE_W0:ROUTE_W0 + 1]
    w1 = route[:, ROUTE_W0 + 1:ROUTE_W0 + 2]
    y = x_ref[...] + w0 * buf0[...] + w1 * buf1[...]
    o_ref[...] = _rms(y, g_ref[...]) if apply_final_norm else y


def _combine_call(pos0, pos1, ys, x2, route, g, apply_final_norm):
    t, d = x2.shape
    tm = ROW_TILE
    n = t // tm

    def ispec():
        return pl.BlockSpec((1, 1, tm), lambda i: (i, 0, 0), memory_space=pltpu.SMEM)

    return pl.pallas_call(
        functools.partial(_combine_kernel, apply_final_norm=apply_final_norm),
        grid=(n,),
        in_specs=[ispec(), ispec(), pl.BlockSpec(memory_space=pl.ANY), _row_spec(tm, d),
                  _row_spec(tm, LANES), _const_spec(g)],
        out_specs=_row_spec(tm, d),
        out_shape=jax.ShapeDtypeStruct((t, d), F32),
        scratch_shapes=[pltpu.VMEM((tm, d), F32), pltpu.VMEM((tm, d), F32),
                        pltpu.SemaphoreType.DMA(()), pltpu.SemaphoreType.DMA(())],
        compiler_params=pltpu.CompilerParams(
            dimension_semantics=("arbitrary",), vmem_limit_bytes=VMEM_LIMIT),
        name="moe_combine",
    )(pos0.reshape(n, 1, tm), pos1.reshape(n, 1, tm), ys, x2, route, g)


def _final_norm_kernel(x_ref, g_ref, o_ref):
    o_ref[...] = _rms(x_ref[...], g_ref[...])


def _final_norm_call(x2, g):
    t, d = x2.shape
    tm = ROW_TILE
    return pl.pallas_call(
        _final_norm_kernel,
        grid=(t // tm,),
        in_specs=[_row_spec(tm, d), _const_spec(g)],
        out_specs=_row_spec(tm, d),
        out_shape=jax.ShapeDtypeStruct((t, d), F32),
        name="final_norm",
    )(x2, g)


def _routing_plan(route, n_tiles):
    t = route.shape[0]
    tm = GROUP_TILE
    experts = route[:, ROUTE_IDX0:ROUTE_IDX0 + TOP_K].astype(jnp.int32).reshape(t * TOP_K)
    onehot = (experts[:, None] == jnp.arange(N_EXPERTS, dtype=jnp.int32)[None, :]).astype(jnp.int32)
    rank = jnp.sum((jnp.cumsum(onehot, axis=0) - onehot) * onehot, axis=1)
    counts = jnp.sum(onehot, axis=0)
    tiles_per = (counts + tm - 1) // tm
    tile_end = jnp.cumsum(tiles_per)
    tile_start = tile_end - tiles_per
    pos = tile_start[experts] * tm + rank
    row_token = jnp.zeros((n_tiles * tm,), jnp.int32).at[pos].set(
        jnp.arange(t * TOP_K, dtype=jnp.int32) // TOP_K)
    tile_ids = jnp.arange(n_tiles, dtype=jnp.int32)
    tile_expert = jnp.minimum(
        jnp.sum((tile_ids[:, None] >= tile_end[None, :]).astype(jnp.int32), axis=1), N_EXPERTS - 1)
    tile_valid = (tile_ids < tile_end[-1]).astype(jnp.int32)
    last_expert = tile_expert[jnp.maximum(tile_end[-1] - 1, 0)]
    tile_expert = jnp.where(tile_valid == 1, tile_expert, last_expert)
    pos2 = pos.reshape(t, TOP_K)
    return tile_expert, tile_valid, row_token, pos2[:, 0], pos2[:, 1]


def _pack_mixer_weights(w_in):
    glr = jnp.pad(w_in[:, OFF_GLR:GATE_OFF], ((0, 0), (0, LANES - GLA_RANK)))
    return jnp.concatenate([w_in[:, :OFF_GLR], glr], axis=1).astype(BF16)


def _row(v):
    return v.reshape(1, -1).astype(F32)


def kernel(x, norm_mix, w_in, b_gate, sgu_w, sgu_b, sgu_norm, conv_w, conv_b, pool_w, pool_scale, gla_wg2, gla_bg, gla_norm, branch_proj, w_out, norm_ffn, ffn_w1, ffn_w3, ffn_w2, moe_router, moe_w1, moe_w3, moe_w2, final_norm):
    b, s, d = x.shape
    t = b * s
    depth = norm_mix.shape[0]
    causal = jnp.tril(jnp.ones((SGU_CHUNK, SGU_CHUNK), dtype=bool))
    out = None
    for l in range(depth):
        w_mix = _pack_mixer_weights(w_in[l])
        sguw = jnp.concatenate([jnp.where(causal, sgu_w[l, hh], 0.0) for hh in range(SGU_HEADS)],
                               axis=1).astype(BF16)
        sgub = jnp.repeat(sgu_b[l].T, SGU_HEAD_DIM, axis=1).astype(F32)
        pw = jax.scipy.linalg.block_diag(*[pool_w[l, gi] for gi in range(len(POOL_WINDOWS))]).astype(BF16)
        wg2 = jnp.pad(gla_wg2[l], ((0, LANES - GLA_RANK), (0, 0))).astype(BF16)
        gn = jnp.tile(gla_norm[l], GLA_HEADS)
        y = _mixer_call(x, _row(norm_mix[l]), w_mix, sguw, sgub, _row(sgu_norm[l]),
                        conv_w[l].astype(F32), _row(conv_b[l]), pw, _row(pool_scale[l]),
                        wg2, _row(gla_bg[l]), _row(gn))
        x2 = _merge_call(x.reshape(t, d), y.reshape(t, N_BRANCH * BRANCH_WIDTH), _row(norm_mix[l]),
                         w_in[l][:, GATE_OFF:].astype(BF16), _row(b_gate[l]),
                         branch_proj[l].reshape(N_BRANCH * BRANCH_WIDTH, d).astype(BF16),
                         w_out[l].astype(BF16))
        i = l // 2
        if l % 2 == 0:
            x2 = _ffn_call(x2, _row(norm_ffn[l]), ffn_w1[i].astype(BF16), ffn_w3[i].astype(BF16),
                           ffn_w2[i].astype(BF16))
            if l == depth - 1:
                out = _final_norm_call(x2, _row(final_norm))
        else:
            router = jnp.pad(moe_router[i], ((0, 0), (0, LANES - N_EXPERTS)))
            rhi = router.astype(BF16)
            rlo = (router - rhi.astype(F32)).astype(BF16)
            h2, route = _router_call(x2, _row(norm_ffn[l]), rhi, rlo)
            n_tiles = (t * TOP_K) // GROUP_TILE + N_EXPERTS
            tile_expert, tile_valid, row_token, pos0, pos1 = _routing_plan(route, n_tiles)
            ys = _group_call(tile_expert, tile_valid, row_token, h2,
                             moe_w1[i].astype(BF16), moe_w3[i].astype(BF16), moe_w2[i].astype(BF16))
            x2 = _combine_call(pos0, pos1, ys, x2, route, _row(final_norm), l == depth - 1)
            if l == depth - 1:
                out = x2
        x = x2.reshape(b, s, d)
    return out.reshape(b, s, d)
```

```python
import functools

import numpy as np
import jax
import jax.numpy as jnp
from jax import lax
from jax.experimental import pallas as pl
from jax.experimental.pallas import tpu as pltpu

F32 = jnp.float32
BF16 = jnp.bfloat16

D_MODEL = 1024
N_BRANCH = 4
BRANCH_WIDTH = 256
SGU_HEADS = 4
SGU_HEAD_DIM = BRANCH_WIDTH // SGU_HEADS
SGU_CHUNK = 128
CONV_WIDTH = 3
POOL_WINDOWS = (2, 4, 8, 16)
POOL_GROUP_DIM = BRANCH_WIDTH // len(POOL_WINDOWS)
POOL_HISTORY = 16
CONV_HISTORY = 8
GLA_HEADS = 4
GLA_DK = 32
GLA_DV = BRANCH_WIDTH // GLA_HEADS
GLA_HK = GLA_HEADS * GLA_DK
GLA_RANK = 16
GLA_TAU = 16.0
GLA_CHUNK = 64
D_FF = 2816
N_EXPERTS = 8
TOP_K = 2
EPS = 1e-6

A_COLS = 2 * BRANCH_WIDTH
B_COLS = 3 * BRANCH_WIDTH
C_COLS = BRANCH_WIDTH
QKVR_COLS = 2 * GLA_HK + 2 * BRANCH_WIDTH
LANES = 128
OFF_A = 0
OFF_B = OFF_A + A_COLS
OFF_C = OFF_B + B_COLS
OFF_D = OFF_C + C_COLS
OFF_GLR = OFF_D + QKVR_COLS
MIX_COLS = OFF_GLR + LANES
GATE_OFF = OFF_GLR + GLA_RANK

MIX_TS = 512
ROW_TILE = 512
GROUP_TILE = 512
FF_CHUNK = 256
VMEM_LIMIT = 56 * 1024 * 1024


def _dot(a, b):
    return jnp.dot(a, b, preferred_element_type=F32)


def _dot_nt(a, b):
    return lax.dot_general(a, b, (((1,), (1,)), ((), ())), preferred_element_type=F32)


def _dot_tn(a, b):
    return lax.dot_general(a, b, (((0,), (0,)), ((), ())), preferred_element_type=F32)


def _split(x):
    hi = x.astype(BF16)
    lo = (x - hi.astype(F32)).astype(BF16)
    return hi, lo


def _rms(x, g):
    ms = jnp.mean(x * x, axis=-1, keepdims=True)
    return x * lax.rsqrt(ms + EPS) * g


def _sigmoid(x):
    return 1.0 / (1.0 + jnp.exp(-x))


def _silu(x):
    return x * _sigmoid(x)


def _gelu_tanh(x):
    c = np.float32(np.sqrt(2.0 / np.pi))
    return x * (0.5 * (1.0 + jnp.tanh(c * (x + 0.044715 * (x * x * x)))))


def _group_mean_matrix(n, group):
    r = lax.broadcasted_iota(jnp.int32, (n, n), 0) // group
    c = lax.broadcasted_iota(jnp.int32, (n, n), 1) // group
    return jnp.where(r == c, 1.0 / group, 0.0).astype(BF16)


def _group_rms(x, bd, g):
    hi, lo = _split(x * x)
    ms = _dot(hi, bd) + _dot(lo, bd)
    return x * lax.rsqrt(ms + EPS) * g


def _mixer_kernel(x_ref, g_ref, w_ref, sguw_ref, sgub_ref, sgun_ref, cw_ref, cb_ref, pw_ref, ps_ref,
                  wg2_ref, bg_ref, gn_ref, y_ref,
                  state_ref, ybuf_ref, zbuf_ref, qd_ref, ki_ref, ke_ref, v_ref, el_ref, o_ref):
    si = pl.program_id(1)
    ts = x_ref.shape[1]
    x = x_ref[0]
    h = _rms(x, g_ref[...]).astype(BF16)

    @pl.when(si == 0)
    def _():
        state_ref[...] = jnp.zeros_like(state_ref)
        ybuf_ref[0:CONV_HISTORY, :] = jnp.zeros((CONV_HISTORY, BRANCH_WIDTH), F32)
        zbuf_ref[0:POOL_HISTORY, :] = jnp.zeros((POOL_HISTORY, BRANCH_WIDTH), F32)

    bd64 = _group_mean_matrix(BRANCH_WIDTH, SGU_HEAD_DIM)
    lane = lax.broadcasted_iota(jnp.int32, (1, BRANCH_WIDTH), 1)

    z = _gelu_tanh(_dot(h, w_ref[:, OFF_A:OFF_A + A_COLS]))
    u = z[:, :BRANCH_WIDTH]
    vn = _group_rms(z[:, BRANCH_WIDTH:], bd64, sgun_ref[...])
    lane_head = lane // SGU_HEAD_DIM
    for c in range(ts // SGU_CHUNK):
        rows = slice(c * SGU_CHUNK, (c + 1) * SGU_CHUNK)
        vc = vn[rows]
        vst = jnp.concatenate([jnp.where(lane_head == hh, vc, 0.0) for hh in range(SGU_HEADS)], axis=0)
        mixed = _dot(sguw_ref[...], vst.astype(BF16)) + sgub_ref[...]
        y_ref[0, rows, 0:BRANCH_WIDTH] = (u[rows] * mixed).astype(BF16)

    zb = _dot(h, w_ref[:, OFF_B:OFF_B + B_COLS])
    gate_b = zb[:, BRANCH_WIDTH:2 * BRANCH_WIDTH]
    yv = zb[:, 2 * BRANCH_WIDTH:] * zb[:, :BRANCH_WIDTH]
    ybuf_ref[CONV_HISTORY:CONV_HISTORY + ts, :] = yv
    conv = cb_ref[...] + yv * cw_ref[CONV_WIDTH - 1:CONV_WIDTH, :]
    for i in range(CONV_WIDTH - 1):
        back = CONV_WIDTH - 1 - i
        conv = conv + ybuf_ref[CONV_HISTORY - back:CONV_HISTORY - back + ts, :] * cw_ref[i:i + 1, :]
    ybuf_ref[0:CONV_HISTORY, :] = ybuf_ref[ts:ts + CONV_HISTORY, :]
    y_ref[0, :, BRANCH_WIDTH:2 * BRANCH_WIDTH] = (gate_b * conv).astype(BF16)

    zc = _dot(h, w_ref[:, OFF_C:OFF_C + C_COLS])
    zbuf_ref[POOL_HISTORY:POOL_HISTORY + ts, :] = zc
    run = zc
    sums = {}
    for j in range(1, POOL_WINDOWS[-1]):
        run = run + zbuf_ref[POOL_HISTORY - j:POOL_HISTORY - j + ts, :]
        if j + 1 in POOL_WINDOWS:
            sums[j + 1] = run
    zbuf_ref[0:POOL_HISTORY, :] = zbuf_ref[ts:ts + POOL_HISTORY, :]
    lane_group = lane // POOL_GROUP_DIM
    wsum = sums[POOL_WINDOWS[-1]]
    win = jnp.full((1, BRANCH_WIDTH), POOL_WINDOWS[-1], jnp.int32)
    for gi in range(len(POOL_WINDOWS) - 2, -1, -1):
        wsum = jnp.where(lane_group == gi, sums[POOL_WINDOWS[gi]], wsum)
        win = jnp.where(lane_group == gi, POOL_WINDOWS[gi], win)
    tpos = si * ts + lax.broadcasted_iota(jnp.int32, (ts, 1), 0)
    count = jnp.minimum(tpos + 1, win).astype(F32)
    pooled = wsum / count - zc
    y_ref[0, :, 2 * BRANCH_WIDTH:3 * BRANCH_WIDTH] = (
        _dot(pooled.astype(BF16), pw_ref[...]) * ps_ref[...]).astype(BF16)

    zd = _dot(h, w_ref[:, OFF_D:OFF_D + QKVR_COLS + LANES])
    q = zd[:, 0:GLA_HK]
    k = zd[:, GLA_HK:2 * GLA_HK]
    v = zd[:, 2 * GLA_HK:2 * GLA_HK + BRANCH_WIDTH]
    r = zd[:, 2 * GLA_HK + BRANCH_WIDTH:QKVR_COLS]
    glr = zd[:, QKVR_COLS:]
    a = _dot(glr.astype(BF16), wg2_ref[...]) + bg_ref[...]
    log_g = (jnp.minimum(a, 0.0) - jnp.log(1.0 + jnp.exp(-jnp.abs(a)))) * (1.0 / GLA_TAU)
    blk = 4 * GLA_CHUNK
    rr = lax.broadcasted_iota(jnp.int32, (blk, blk), 0)
    cc = lax.broadcasted_iota(jnp.int32, (blk, blk), 1)
    same = (rr // GLA_CHUNK) == (cc // GLA_CHUNK)
    tri = jnp.where(same & (cc <= rr), 1.0, 0.0).astype(BF16)
    ones = jnp.where(same, 1.0, 0.0).astype(BF16)
    cums, lasts = [], []
    for b in range(ts // blk):
        hi, lo = _split(log_g[b * blk:(b + 1) * blk])
        cums.append(_dot(tri, hi) + _dot(tri, lo))
        lasts.append(_dot(ones, hi) + _dot(ones, lo))
    cum = jnp.concatenate(cums, axis=0)
    last = jnp.concatenate(lasts, axis=0)
    qd_ref[...] = (q * (GLA_DK ** -0.5) * jnp.exp(cum)).astype(BF16)
    ki_ref[...] = (k * jnp.exp(-cum)).astype(BF16)
    ke_ref[...] = (k * jnp.exp(last - cum)).astype(BF16)
    v_ref[...] = v.astype(BF16)
    el_ref[...] = jnp.exp(last)

    nrow = GLA_HEADS * GLA_CHUNK
    row_head = lax.broadcasted_iota(jnp.int32, (nrow, 1), 0) // GLA_CHUNK
    mask_k = jnp.where(row_head == lax.broadcasted_iota(jnp.int32, (nrow, GLA_HK), 1) // GLA_DK,
                       1.0, 0.0).astype(BF16)
    mask_v = jnp.where(row_head == lax.broadcasted_iota(jnp.int32, (nrow, BRANCH_WIDTH), 1) // GLA_DV,
                       1.0, 0.0).astype(BF16)
    causal = (lax.broadcasted_iota(jnp.int32, (GLA_CHUNK, nrow), 1) % GLA_CHUNK
              <= lax.broadcasted_iota(jnp.int32, (GLA_CHUNK, nrow), 0))
    mask_s = (lax.broadcasted_iota(jnp.int32, (BRANCH_WIDTH, GLA_HK), 0) // GLA_DV
              == lax.broadcasted_iota(jnp.int32, (BRANCH_WIDTH, GLA_HK), 1) // GLA_DK)

    def chunk_step(n, carry):
        rows = pl.ds(pl.multiple_of(n * GLA_CHUNK, GLA_CHUNK), GLA_CHUNK)
        qd = qd_ref[rows, :]
        ki = ki_ref[rows, :]
        ke = ke_ref[rows, :]
        vc = v_ref[rows, :]
        kbd = jnp.concatenate([ki] * GLA_HEADS, axis=0) * mask_k
        vbd = jnp.concatenate([vc] * GLA_HEADS, axis=0) * mask_v
        scores = jnp.where(causal, _dot_nt(qd, kbd), 0.0)
        state = state_ref[...]
        o_ref[rows, :] = _dot(scores.astype(BF16), vbd) + _dot_nt(qd, state.astype(BF16))
        decay = el_ref[pl.ds(pl.multiple_of(n * GLA_CHUNK, GLA_CHUNK), 1), :]
        kv = _dot_tn(vc, ke)
        state_ref[...] = state * decay + jnp.where(mask_s, kv, 0.0)
        return carry

    lax.fori_loop(0, ts // GLA_CHUNK, chunk_step, 0)
    on = _group_rms(o_ref[...], bd64, gn_ref[...])
    y_ref[0, :, 3 * BRANCH_WIDTH:] = (_silu(r) * on).astype(BF16)


def _mixer_call(x, g, w_mix, sguw, sgub, sgun, cw, cb, pw, ps, wg2, bg, gn):
    b, s, d = x.shape
    ts = MIX_TS

    def const(arr):
        return pl.BlockSpec(arr.shape, lambda bi, si: (0,) * arr.ndim)

    consts = (g, w_mix, sguw, sgub, sgun, cw, cb, pw, ps, wg2, bg, gn)
    return pl.pallas_call(
        _mixer_kernel,
        grid=(b, s // ts),
        in_specs=[pl.BlockSpec((1, ts, d), lambda bi, si: (bi, si, 0))] + [const(c) for c in consts],
        out_specs=pl.BlockSpec((1, ts, N_BRANCH * BRANCH_WIDTH), lambda bi, si: (bi, si, 0)),
        out_shape=jax.ShapeDtypeStruct((b, s, N_BRANCH * BRANCH_WIDTH), BF16),
        scratch_shapes=[
            pltpu.VMEM((BRANCH_WIDTH, GLA_HK), F32),
            pltpu.VMEM((CONV_HISTORY + ts, BRANCH_WIDTH), F32),
            pltpu.VMEM((POOL_HISTORY + ts, BRANCH_WIDTH), F32),
            pltpu.VMEM((ts, GLA_HK), BF16),
            pltpu.VMEM((ts, GLA_HK), BF16),
            pltpu.VMEM((ts, GLA_HK), BF16),
            pltpu.VMEM((ts, BRANCH_WIDTH), BF16),
            pltpu.VMEM((ts, GLA_HK), F32),
            pltpu.VMEM((ts, BRANCH_WIDTH), F32),
        ],
        compiler_params=pltpu.CompilerParams(
            dimension_semantics=("arbitrary", "arbitrary"), vmem_limit_bytes=VMEM_LIMIT),
        name="mixer",
    )(x, *consts)


def _merge_kernel(x_ref, y_ref, g_ref, wg_ref, bgate_ref, p_ref, wo_ref, o_ref):
    x = x_ref[...]
    h = _rms(x, g_ref[...]).astype(BF16)
    merged = None
    for i in range(N_BRANCH):
        cols = slice(i * D_MODEL, (i + 1) * D_MODEL)
        gate = _sigmoid(_dot(h, wg_ref[:, cols]) + bgate_ref[:, cols])
        rows = slice(i * BRANCH_WIDTH, (i + 1) * BRANCH_WIDTH)
        term = gate * _dot(y_ref[:, rows], p_ref[rows, :])
        merged = term if merged is None else merged + term
    o_ref[...] = x + _dot(merged.astype(BF16), wo_ref[...])


def _row_spec(tm, width):
    return pl.BlockSpec((tm, width), lambda i: (i, 0))


def _const_spec(arr):
    return pl.BlockSpec(arr.shape, lambda i: (0,) * arr.ndim)


def _merge_call(x2, y2, g, wg, bgate, p, wo):
    t, d = x2.shape
    tm = ROW_TILE
    consts = (g, wg, bgate, p, wo)
    return pl.pallas_call(
        _merge_kernel,
        grid=(t // tm,),
        in_specs=[_row_spec(tm, d), _row_spec(tm, y2.shape[1])] + [_const_spec(c) for c in consts],
        out_specs=_row_spec(tm, d),
        out_shape=jax.ShapeDtypeStruct((t, d), F32),
        compiler_params=pltpu.CompilerParams(
            dimension_semantics=("arbitrary",), vmem_limit_bytes=VMEM_LIMIT),
        name="merge",
    )(x2, y2, *consts)


def _swiglu_rows(hb, w1_ref, w3_ref, w2_ref, lead):
    acc = None
    for f in range(D_FF // FF_CHUNK):
        cols = slice(f * FF_CHUNK, (f + 1) * FF_CHUNK)
        up = _dot(hb, w1_ref[lead + (slice(None), cols)])
        gt = _dot(hb, w3_ref[lead + (slice(None), cols)])
        part = _dot((_silu(up) * gt).astype(BF16), w2_ref[lead + (cols, slice(None))])
        acc = part if acc is None else acc + part
    return acc


def _ffn_kernel(x_ref, g_ref, w1_ref, w3_ref, w2_ref, o_ref):
    x = x_ref[...]
    hb = _rms(x, g_ref[...]).astype(BF16)
    o_ref[...] = x + _swiglu_rows(hb, w1_ref, w3_ref, w2_ref, ())


def _ffn_call(x2, g, w1, w3, w2):
    t, d = x2.shape
    tm = ROW_TILE
    consts = (g, w1, w3, w2)
    return pl.pallas_call(
        _ffn_kernel,
        grid=(t // tm,),
        in_specs=[_row_spec(tm, d)] + [_const_spec(c) for c in consts],
        out_specs=_row_spec(tm, d),
        out_shape=jax.ShapeDtypeStruct((t, d), F32),
        compiler_params=pltpu.CompilerParams(
            dimension_semantics=("arbitrary",), vmem_limit_bytes=VMEM_LIMIT),
        name="ffn",
    )(x2, *consts)


ROUTE_IDX0 = N_EXPERTS
ROUTE_W0 = N_EXPERTS + TOP_K
D_TILES = D_MODEL // LANES


def _to_token_tiles(ref, value, lead=()):
    rows = value.shape[0]
    for c in range(D_TILES):
        ref[lead + (pl.ds(c, rows, stride=D_TILES), slice(None))] = value[:, c * LANES:(c + 1) * LANES]


def _from_token_tiles(ref, rows, lead=()):
    return jnp.concatenate(
        [ref[lead + (pl.ds(c, rows, stride=D_TILES), slice(None))] for c in range(D_TILES)], axis=1)


def _router_kernel(x_ref, g_ref, rhi_ref, rlo_ref, h_ref, route_ref):
    x = x_ref[...]
    h = _rms(x, g_ref[...])
    _to_token_tiles(h_ref, h)
    hi, lo = _split(h)
    logits = _dot(hi, rhi_ref[...]) + _dot(hi, rlo_ref[...]) + _dot(lo, rhi_ref[...])
    lane = lax.broadcasted_iota(jnp.int32, logits.shape, 1)
    neg = jnp.float32(-jnp.inf)
    logits = jnp.where(lane < N_EXPERTS, logits, neg)
    m1 = jnp.max(logits, axis=-1, keepdims=True)
    i1 = jnp.min(jnp.where(logits == m1, lane, LANES), axis=-1, keepdims=True)
    rest = jnp.where(lane == i1, neg, logits)
    m2 = jnp.max(rest, axis=-1, keepdims=True)
    i2 = jnp.min(jnp.where(rest == m2, lane, LANES), axis=-1, keepdims=True)
    e2 = jnp.exp(m2 - m1)
    w1 = 1.0 / (1.0 + e2)
    w2 = e2 / (1.0 + e2)
    out = jnp.where(lane == ROUTE_IDX0, i1.astype(F32), 0.0)
    out = jnp.where(lane == ROUTE_IDX0 + 1, i2.astype(F32), out)
    out = jnp.where(lane == ROUTE_W0, w1, out)
    out = jnp.where(lane == ROUTE_W0 + 1, w2, out)
    route_ref[...] = out


def _router_call(x2, g, rhi, rlo):
    t, d = x2.shape
    tm = ROW_TILE
    consts = (g, rhi, rlo)
    return pl.pallas_call(
        _router_kernel,
        grid=(t // tm,),
        in_specs=[_row_spec(tm, d)] + [_const_spec(c) for c in consts],
        out_specs=[_row_spec(tm * D_TILES, LANES), _row_spec(tm, LANES)],
        out_shape=[jax.ShapeDtypeStruct((t * D_TILES, LANES), F32), jax.ShapeDtypeStruct((t, LANES), F32)],
        compiler_params=pltpu.CompilerParams(
            dimension_semantics=("arbitrary",), vmem_limit_bytes=VMEM_LIMIT),
        name="router",
    )(x2, *consts)


GATHER_UNROLL = 8


def _token_tile(ref, r):
    return ref.at[pl.ds(pl.multiple_of(r * D_TILES, D_TILES), D_TILES), :]


def _start_row_gather(idx_ref, src_hbm, dst, sem):
    def body(r, c):
        pltpu.make_async_copy(_token_tile(src_hbm, idx_ref[0, 0, r]), _token_tile(dst, r), sem).start()
        return c

    lax.fori_loop(0, dst.shape[0] // D_TILES, body, 0, unroll=GATHER_UNROLL)


def _wait_row_gather(src_hbm, dst, sem):
    pltpu.make_async_copy(src_hbm.at[pl.ds(0, dst.shape[0]), :], dst, sem).wait()


def _group_kernel(texp_ref, tvalid_ref, rows_ref, rows_next_ref, h_hbm, w1_ref, w3_ref, w2_ref, ys_ref,
                  xbuf, sem):
    i = pl.program_id(0)
    slot = i % 2

    @pl.when(i == 0)
    def _():
        _start_row_gather(rows_ref, h_hbm, xbuf.at[0], sem.at[0])

    @pl.when(i + 1 < pl.num_programs(0))
    def _():
        _start_row_gather(rows_next_ref, h_hbm, xbuf.at[1 - slot], sem.at[1 - slot])

    _wait_row_gather(h_hbm, xbuf.at[slot], sem.at[slot])

    @pl.when(tvalid_ref[i] == 1)
    def _():
        xb = _from_token_tiles(xbuf, GROUP_TILE, (slot,)).astype(BF16)
        _to_token_tiles(ys_ref, _swiglu_rows(xb, w1_ref, w3_ref, w2_ref, (0,)))

    @pl.when(tvalid_ref[i] == 0)
    def _():
        ys_ref[...] = jnp.zeros_like(ys_ref)


def _group_call(tile_expert, tile_valid, row_token, h3, w1, w3, w2):
    n_tiles = tile_expert.shape[0]
    tm = GROUP_TILE
    rows = row_token.reshape(n_tiles, 1, tm)

    def wspec(arr):
        return pl.BlockSpec((1,) + arr.shape[1:], lambda i, te, tv: (te[i], 0, 0))

    grid_spec = pltpu.PrefetchScalarGridSpec(
        num_scalar_prefetch=2,
        grid=(n_tiles,),
        in_specs=[
            pl.BlockSpec((1, 1, tm), lambda i, te, tv: (i, 0, 0), memory_space=pltpu.SMEM),
            pl.BlockSpec((1, 1, tm), lambda i, te, tv: (jnp.minimum(i + 1, n_tiles - 1), 0, 0),
                         memory_space=pltpu.SMEM),
            pl.BlockSpec(memory_space=pl.ANY),
            wspec(w1), wspec(w3), wspec(w2),
        ],
        out_specs=pl.BlockSpec((tm * D_TILES, LANES), lambda i, te, tv: (i, 0)),
        scratch_shapes=[pltpu.VMEM((2, tm * D_TILES, LANES), F32), pltpu.SemaphoreType.DMA((2,))],
    )
    return pl.pallas_call(
        _group_kernel,
        grid_spec=grid_spec,
        out_shape=jax.ShapeDtypeStruct((n_tiles * tm * D_TILES, LANES), F32),
        compiler_params=pltpu.CompilerParams(
            dimension_semantics=("arbitrary",), vmem_limit_bytes=VMEM_LIMIT),
        name="moe_group",
    )(tile_expert, tile_valid, rows, rows, h3, w1, w3, w2)


def _combine_kernel(p0_ref, p1_ref, p0_next_ref, p1_next_ref, ys_hbm, x_ref, route_ref, g_ref, o_ref,
                    buf, sem, *, apply_final_norm):
    i = pl.program_id(0)
    slot = i % 2

    def start(refs, s):
        for j in range(TOP_K):
            _start_row_gather(refs[j], ys_hbm, buf.at[s, j], sem.at[s, j])

    @pl.when(i == 0)
    def _():
        start((p0_ref, p1_ref), 0)

    @pl.when(i + 1 < pl.num_programs(0))
    def _():
        start((p0_next_ref, p1_next_ref), 1 - slot)

    for j in range(TOP_K):
        _wait_row_gather(ys_hbm, buf.at[slot, j], sem.at[slot, j])
    route = route_ref[...]
    y = x_ref[...]
    for j in range(TOP_K):
        y = y + route[:, ROUTE_W0 + j:ROUTE_W0 + j + 1] * _from_token_tiles(buf, ROW_TILE, (slot, j))
    o_ref[...] = _rms(y, g_ref[...]) if apply_final_norm else y


def _combine_call(pos0, pos1, ys, x2, route, g, apply_final_norm):
    t, d = x2.shape
    tm = ROW_TILE
    n = t // tm

    def ispec(shift):
        return pl.BlockSpec((1, 1, tm), lambda i: (jnp.minimum(i + shift, n - 1), 0, 0),
                            memory_space=pltpu.SMEM)

    p0 = pos0.reshape(n, 1, tm)
    p1 = pos1.reshape(n, 1, tm)
    return pl.pallas_call(
        functools.partial(_combine_kernel, apply_final_norm=apply_final_norm),
        grid=(n,),
        in_specs=[ispec(0), ispec(0), ispec(1), ispec(1), pl.BlockSpec(memory_space=pl.ANY),
                  _row_spec(tm, d), _row_spec(tm, LANES), _const_spec(g)],
        out_specs=_row_spec(tm, d),
        out_shape=jax.ShapeDtypeStruct((t, d), F32),
        scratch_shapes=[pltpu.VMEM((2, TOP_K, tm * D_TILES, LANES), F32),
                        pltpu.SemaphoreType.DMA((2, TOP_K))],
        compiler_params=pltpu.CompilerParams(
            dimension_semantics=("arbitrary",), vmem_limit_bytes=VMEM_LIMIT),
        name="moe_combine",
    )(p0, p1, p0, p1, ys, x2, route, g)


def _final_norm_kernel(x_ref, g_ref, o_ref):
    o_ref[...] = _rms(x_ref[...], g_ref[...])


def _final_norm_call(x2, g):
    t, d = x2.shape
    tm = ROW_TILE
    return pl.pallas_call(
        _final_norm_kernel,
        grid=(t // tm,),
        in_specs=[_row_spec(tm, d), _const_spec(g)],
        out_specs=_row_spec(tm, d),
        out_shape=jax.ShapeDtypeStruct((t, d), F32),
        name="final_norm",
    )(x2, g)


def _routing_plan(route, n_tiles):
    t = route.shape[0]
    tm = GROUP_TILE
    experts = route[:, ROUTE_IDX0:ROUTE_IDX0 + TOP_K].astype(jnp.int32).reshape(t * TOP_K)
    onehot = (experts[:, None] == jnp.arange(N_EXPERTS, dtype=jnp.int32)[None, :]).astype(jnp.int32)
    rank = jnp.sum((jnp.cumsum(onehot, axis=0) - onehot) * onehot, axis=1)
    counts = jnp.sum(onehot, axis=0)
    tiles_per = (counts + tm - 1) // tm
    tile_end = jnp.cumsum(tiles_per)
    tile_start = tile_end - tiles_per
    pos = tile_start[experts] * tm + rank
    tile_ids = jnp.arange(n_tiles, dtype=jnp.int32)
    tile_expert = jnp.minimum(
        jnp.sum((tile_ids[:, None] >= tile_end[None, :]).astype(jnp.int32), axis=1), N_EXPERTS - 1)
    tile_valid = (tile_ids < tile_end[-1]).astype(jnp.int32)
    order = jnp.argsort(experts, stable=True).astype(jnp.int32)
    row_expert = jnp.repeat(tile_expert, tm)
    row_rank = jnp.arange(n_tiles * tm, dtype=jnp.int32) - tile_start[row_expert] * tm
    row_used = (row_rank < counts[row_expert]) & (jnp.repeat(tile_valid, tm) == 1)
    src = jnp.clip((jnp.cumsum(counts) - counts)[row_expert] + row_rank, 0, t * TOP_K - 1)
    row_token = jnp.where(row_used, order[src] // TOP_K, 0)
    last_expert = tile_expert[jnp.maximum(tile_end[-1] - 1, 0)]
    tile_expert = jnp.where(tile_valid == 1, tile_expert, last_expert)
    pos2 = pos.reshape(t, TOP_K)
    return tile_expert, tile_valid, row_token, pos2[:, 0], pos2[:, 1]


def _pack_mixer_weights(w_in):
    glr = jnp.pad(w_in[:, OFF_GLR:GATE_OFF], ((0, 0), (0, LANES - GLA_RANK)))
    return jnp.concatenate([w_in[:, :OFF_GLR], glr], axis=1).astype(BF16)


def _row(v):
    return v.reshape(1, -1).astype(F32)


def kernel(x, norm_mix, w_in, b_gate, sgu_w, sgu_b, sgu_norm, conv_w, conv_b, pool_w, pool_scale, gla_wg2, gla_bg, gla_norm, branch_proj, w_out, norm_ffn, ffn_w1, ffn_w3, ffn_w2, moe_router, moe_w1, moe_w3, moe_w2, final_norm):
    b, s, d = x.shape
    t = b * s
    depth = norm_mix.shape[0]
    causal = jnp.tril(jnp.ones((SGU_CHUNK, SGU_CHUNK), dtype=bool))
    out = None
    for l in range(depth):
        w_mix = _pack_mixer_weights(w_in[l])
        sguw = jnp.concatenate([jnp.where(causal, sgu_w[l, hh], 0.0) for hh in range(SGU_HEADS)],
                               axis=1).astype(BF16)
        sgub = jnp.repeat(sgu_b[l].T, SGU_HEAD_DIM, axis=1).astype(F32)
        pw = jax.scipy.linalg.block_diag(*[pool_w[l, gi] for gi in range(len(POOL_WINDOWS))]).astype(BF16)
        wg2 = jnp.pad(gla_wg2[l], ((0, LANES - GLA_RANK), (0, 0))).astype(BF16)
        gn = jnp.tile(gla_norm[l], GLA_HEADS)
        y = _mixer_call(x, _row(norm_mix[l]), w_mix, sguw, sgub, _row(sgu_norm[l]),
                        conv_w[l].astype(F32), _row(conv_b[l]), pw, _row(pool_scale[l]),
                        wg2, _row(gla_bg[l]), _row(gn))
        x2 = _merge_call(x.reshape(t, d), y.reshape(t, N_BRANCH * BRANCH_WIDTH), _row(norm_mix[l]),
                         w_in[l][:, GATE_OFF:].astype(BF16), _row(b_gate[l]),
                         branch_proj[l].reshape(N_BRANCH * BRANCH_WIDTH, d).astype(BF16),
                         w_out[l].astype(BF16))
        i = l // 2
        if l % 2 == 0:
            x2 = _ffn_call(x2, _row(norm_ffn[l]), ffn_w1[i].astype(BF16), ffn_w3[i].astype(BF16),
                           ffn_w2[i].astype(BF16))
            if l == depth - 1:
                out = _final_norm_call(x2, _row(final_norm))
        else:
            router = jnp.pad(moe_router[i], ((0, 0), (0, LANES - N_EXPERTS)))
            rhi = router.astype(BF16)
            rlo = (router - rhi.astype(F32)).astype(BF16)
            h2, route = _router_call(x2, _row(norm_ffn[l]), rhi, rlo)
            n_tiles = (t * TOP_K) // GROUP_TILE + N_EXPERTS
            tile_expert, tile_valid, row_token, pos0, pos1 = _routing_plan(route, n_tiles)
            ys = _group_call(tile_expert, tile_valid, row_token, h2,
                             moe_w1[i].astype(BF16), moe_w3[i].astype(BF16), moe_w2[i].astype(BF16))
            x2 = _combine_call(pos0, pos1, ys, x2, route, _row(final_norm), l == depth - 1)
            if l == depth - 1:
                out = x2
        x = x2.reshape(b, s, d)
    return out.reshape(b, s, d)
```

```python
import functools

import numpy as np
import jax
import jax.numpy as jnp
from jax import lax
from jax.experimental import pallas as pl
from jax.experimental.pallas import tpu as pltpu

F32 = jnp.float32
BF16 = jnp.bfloat16

D_MODEL = 1024
N_BRANCH = 4
BRANCH_WIDTH = 256
SGU_HEADS = 4
SGU_HEAD_DIM = BRANCH_WIDTH // SGU_HEADS
SGU_CHUNK = 128
CONV_WIDTH = 3
POOL_WINDOWS = (2, 4, 8, 16)
POOL_GROUP_DIM = BRANCH_WIDTH // len(POOL_WINDOWS)
POOL_HISTORY = 16
POOL_PAD = 8
assert POOL_WINDOWS == (2, 4, 8, 16)
CONV_HISTORY = 8
GLA_HEADS = 4
GLA_DK = 32
GLA_DV = BRANCH_WIDTH // GLA_HEADS
GLA_HK = GLA_HEADS * GLA_DK
GLA_RANK = 16
GLA_TAU = 16.0
GLA_CHUNK = 64
D_FF = 2816
N_EXPERTS = 8
TOP_K = 2
EPS = 1e-6

A_COLS = 2 * BRANCH_WIDTH
B_COLS = 3 * BRANCH_WIDTH
C_COLS = BRANCH_WIDTH
QKVR_COLS = 2 * GLA_HK + 2 * BRANCH_WIDTH
LANES = 128
OFF_A = 0
OFF_B = OFF_A + A_COLS
OFF_C = OFF_B + B_COLS
OFF_D = OFF_C + C_COLS
OFF_GLR = OFF_D + QKVR_COLS
MIX_COLS = OFF_GLR + LANES
GATE_OFF = OFF_GLR + GLA_RANK

MIX_TS = 512
ROW_TILE = 512
GROUP_TILE = 512
FF_CHUNK = 256
VMEM_LIMIT = 56 * 1024 * 1024


def _dot(a, b):
    return jnp.dot(a, b, preferred_element_type=F32)


def _dot_nt(a, b):
    return lax.dot_general(a, b, (((1,), (1,)), ((), ())), preferred_element_type=F32)


def _dot_tn(a, b):
    return lax.dot_general(a, b, (((0,), (0,)), ((), ())), preferred_element_type=F32)


def _split(x):
    hi = x.astype(BF16)
    lo = (x - hi.astype(F32)).astype(BF16)
    return hi, lo


def _rms(x, g):
    ms = jnp.mean(x * x, axis=-1, keepdims=True)
    return x * lax.rsqrt(ms + EPS) * g


def _sigmoid(x):
    return 1.0 / (1.0 + jnp.exp(-x))


def _silu(x):
    return x * _sigmoid(x)


def _gelu_tanh(x):
    c = np.float32(np.sqrt(2.0 / np.pi))
    return x * (0.5 * (1.0 + jnp.tanh(c * (x + 0.044715 * (x * x * x)))))


def _group_mean_matrix(n, group):
    r = lax.broadcasted_iota(jnp.int32, (n, n), 0) // group
    c = lax.broadcasted_iota(jnp.int32, (n, n), 1) // group
    return jnp.where(r == c, 1.0 / group, 0.0).astype(BF16)


def _group_rms(x, bd, g):
    hi, lo = _split(x * x)
    ms = _dot(hi, bd) + _dot(lo, bd)
    return x * lax.rsqrt(ms + EPS) * g


def _mixer_kernel(x_ref, g_ref, w_ref, sguw_ref, sgub_ref, sgun_ref, cw_ref, cb_ref, pw_ref, ps_ref,
                  wg2_ref, bg_ref, gn_ref, y_ref,
                  state_ref, ybuf_ref, zbuf_ref):
    si = pl.program_id(1)
    ts = x_ref.shape[1]
    x = x_ref[0]
    h = _rms(x, g_ref[...]).astype(BF16)

    @pl.when(si == 0)
    def _():
        state_ref[...] = jnp.zeros_like(state_ref)
        ybuf_ref[0:CONV_HISTORY, :] = jnp.zeros((CONV_HISTORY, BRANCH_WIDTH), F32)
        zbuf_ref[:, 0:POOL_PAD + POOL_HISTORY, :] = jnp.zeros((3, POOL_PAD + POOL_HISTORY, BRANCH_WIDTH), F32)

    bd64 = _group_mean_matrix(BRANCH_WIDTH, SGU_HEAD_DIM)
    lane = lax.broadcasted_iota(jnp.int32, (1, BRANCH_WIDTH), 1)

    z = _gelu_tanh(_dot(h, w_ref[:, OFF_A:OFF_A + A_COLS]))
    u = z[:, :BRANCH_WIDTH]
    vn = _group_rms(z[:, BRANCH_WIDTH:], bd64, sgun_ref[...])
    lane_head = lane // SGU_HEAD_DIM
    vnb = vn.astype(BF16)
    for c in range(ts // SGU_CHUNK):
        rows = slice(c * SGU_CHUNK, (c + 1) * SGU_CHUNK)
        stacked = _dot(sguw_ref[...], vnb[rows])
        mixed = stacked[(SGU_HEADS - 1) * SGU_CHUNK:]
        for hh in range(SGU_HEADS - 2, -1, -1):
            mixed = jnp.where(lane_head == hh, stacked[hh * SGU_CHUNK:(hh + 1) * SGU_CHUNK], mixed)
        y_ref[0, rows, 0:BRANCH_WIDTH] = (u[rows] * (mixed + sgub_ref[...])).astype(BF16)

    zb = _dot(h, w_ref[:, OFF_B:OFF_B + B_COLS])
    gate_b = zb[:, BRANCH_WIDTH:2 * BRANCH_WIDTH]
    yv = zb[:, 2 * BRANCH_WIDTH:] * zb[:, :BRANCH_WIDTH]
    ybuf_ref[CONV_HISTORY:CONV_HISTORY + ts, :] = yv
    conv = cb_ref[...] + yv * cw_ref[CONV_WIDTH - 1:CONV_WIDTH, :]
    for i in range(CONV_WIDTH - 1):
        back = CONV_WIDTH - 1 - i
        conv = conv + ybuf_ref[CONV_HISTORY - back:CONV_HISTORY - back + ts, :] * cw_ref[i:i + 1, :]
    ybuf_ref[0:CONV_HISTORY, :] = ybuf_ref[ts:ts + CONV_HISTORY, :]
    y_ref[0, :, BRANCH_WIDTH:2 * BRANCH_WIDTH] = (gate_b * conv).astype(BF16)

    zc = _dot(h, w_ref[:, OFF_C:OFF_C + C_COLS])
    ext = POOL_HISTORY + ts
    p0 = POOL_PAD
    zbuf_ref[0, p0 + POOL_HISTORY:p0 + ext, :] = zc
    s2 = zbuf_ref[0, p0:p0 + ext, :] + zbuf_ref[0, p0 - 1:p0 - 1 + ext, :]
    zbuf_ref[1, p0:p0 + ext, :] = s2
    s4 = s2 + zbuf_ref[1, p0 - 2:p0 - 2 + ext, :]
    zbuf_ref[2, p0:p0 + ext, :] = s4
    s8 = s4 + zbuf_ref[2, p0 - 4:p0 - 4 + ext, :]
    sums = {2: s2[POOL_HISTORY:], 4: s4[POOL_HISTORY:], 8: s8[POOL_HISTORY:],
            16: s8[POOL_HISTORY:] + s8[POOL_HISTORY - 8:POOL_HISTORY - 8 + ts]}
    zbuf_ref[0, p0:p0 + POOL_HISTORY, :] = zbuf_ref[0, p0 + ts:p0 + ext, :]
    lane_group = lane // POOL_GROUP_DIM
    wsum = sums[POOL_WINDOWS[-1]]
    win = jnp.full((1, BRANCH_WIDTH), POOL_WINDOWS[-1], jnp.int32)
    for gi in range(len(POOL_WINDOWS) - 2, -1, -1):
        wsum = jnp.where(lane_group == gi, sums[POOL_WINDOWS[gi]], wsum)
        win = jnp.where(lane_group == gi, POOL_WINDOWS[gi], win)
    tpos = si * ts + lax.broadcasted_iota(jnp.int32, (ts, 1), 0)
    count = jnp.minimum(tpos + 1, win).astype(F32)
    pooled = wsum / count - zc
    y_ref[0, :, 2 * BRANCH_WIDTH:3 * BRANCH_WIDTH] = (
        _dot(pooled.astype(BF16), pw_ref[...]) * ps_ref[...]).astype(BF16)

    zd = _dot(h, w_ref[:, OFF_D:OFF_D + QKVR_COLS + LANES])
    q = zd[:, 0:GLA_HK]
    k = zd[:, GLA_HK:2 * GLA_HK]
    v = zd[:, 2 * GLA_HK:2 * GLA_HK + BRANCH_WIDTH]
    r = zd[:, 2 * GLA_HK + BRANCH_WIDTH:QKVR_COLS]
    glr = zd[:, QKVR_COLS:]
    a = _dot(glr.astype(BF16), wg2_ref[...]) + bg_ref[...]
    log_g = (jnp.minimum(a, 0.0) - jnp.log(1.0 + jnp.exp(-jnp.abs(a)))) * (1.0 / GLA_TAU)
    blk = 4 * GLA_CHUNK
    rr = lax.broadcasted_iota(jnp.int32, (blk, blk), 0)
    cc = lax.broadcasted_iota(jnp.int32, (blk, blk), 1)
    same = (rr // GLA_CHUNK) == (cc // GLA_CHUNK)
    tri = jnp.where(same & (cc <= rr), 1.0, 0.0).astype(BF16)
    ones = jnp.where(same, 1.0, 0.0).astype(BF16)
    cums, lasts = [], []
    for b in range(ts // blk):
        hi, lo = _split(log_g[b * blk:(b + 1) * blk])
        cums.append(_dot(tri, hi) + _dot(tri, lo))
        lasts.append(_dot(ones, hi) + _dot(ones, lo))
    cum = jnp.concatenate(cums, axis=0)
    last = jnp.concatenate(lasts, axis=0)
    qd_all = (q * (GLA_DK ** -0.5) * jnp.exp(cum)).astype(BF16)
    ki_all = (k * jnp.exp(-cum)).astype(BF16)
    ke_all = (k * jnp.exp(last - cum)).astype(BF16)
    v_all = v.astype(BF16)
    decay_all = jnp.exp(last)

    nrow = GLA_HEADS * GLA_CHUNK
    row_head = lax.broadcasted_iota(jnp.int32, (nrow, 1), 0) // GLA_CHUNK
    mask_k = jnp.where(row_head == lax.broadcasted_iota(jnp.int32, (nrow, GLA_HK), 1) // GLA_DK,
                       1.0, 0.0).astype(BF16)
    mask_v = jnp.where(row_head == lax.broadcasted_iota(jnp.int32, (nrow, BRANCH_WIDTH), 1) // GLA_DV,
                       1.0, 0.0).astype(BF16)
    causal = (lax.broadcasted_iota(jnp.int32, (GLA_CHUNK, nrow), 1) % GLA_CHUNK
              <= lax.broadcasted_iota(jnp.int32, (GLA_CHUNK, nrow), 0))
    mask_s = (lax.broadcasted_iota(jnp.int32, (BRANCH_WIDTH, GLA_HK), 0) // GLA_DV
              == lax.broadcasted_iota(jnp.int32, (BRANCH_WIDTH, GLA_HK), 1) // GLA_DK)

    state = state_ref[...]
    outs = []
    for n in range(ts // GLA_CHUNK):
        rows = slice(n * GLA_CHUNK, (n + 1) * GLA_CHUNK)
        qd = qd_all[rows]
        vc = v_all[rows]
        kbd = jnp.concatenate([ki_all[rows]] * GLA_HEADS, axis=0) * mask_k
        vbd = jnp.concatenate([vc] * GLA_HEADS, axis=0) * mask_v
        scores = jnp.where(causal, _dot_nt(qd, kbd), 0.0)
        outs.append(_dot(scores.astype(BF16), vbd) + _dot_nt(qd, state.astype(BF16)))
        kv = _dot_tn(vc, ke_all[rows])
        state = state * decay_all[n * GLA_CHUNK:n * GLA_CHUNK + 1, :] + jnp.where(mask_s, kv, 0.0)
    state_ref[...] = state
    on = _group_rms(jnp.concatenate(outs, axis=0), bd64, gn_ref[...])
    y_ref[0, :, 3 * BRANCH_WIDTH:] = (_silu(r) * on).astype(BF16)


def _mixer_call(x, g, w_mix, sguw, sgub, sgun, cw, cb, pw, ps, wg2, bg, gn):
    b, s, d = x.shape
    ts = MIX_TS

    def const(arr):
        return pl.BlockSpec(arr.shape, lambda bi, si: (0,) * arr.ndim)

    consts = (g, w_mix, sguw, sgub, sgun, cw, cb, pw, ps, wg2, bg, gn)
    return pl.pallas_call(
        _mixer_kernel,
        grid=(b, s // ts),
        in_specs=[pl.BlockSpec((1, ts, d), lambda bi, si: (bi, si, 0))] + [const(c) for c in consts],
        out_specs=pl.BlockSpec((1, ts, N_BRANCH * BRANCH_WIDTH), lambda bi, si: (bi, si, 0)),
        out_shape=jax.ShapeDtypeStruct((b, s, N_BRANCH * BRANCH_WIDTH), BF16),
        scratch_shapes=[
            pltpu.VMEM((BRANCH_WIDTH, GLA_HK), F32),
            pltpu.VMEM((CONV_HISTORY + ts, BRANCH_WIDTH), F32),
            pltpu.VMEM((3, POOL_PAD + POOL_HISTORY + ts, BRANCH_WIDTH), F32),
        ],
        compiler_params=pltpu.CompilerParams(
            dimension_semantics=("arbitrary", "arbitrary"), vmem_limit_bytes=VMEM_LIMIT),
        name="mixer",
    )(x, *consts)


def _merge_kernel(x_ref, y_ref, g_ref, wg_ref, bgate_ref, p_ref, wo_ref, o_ref):
    x = x_ref[...]
    h = _rms(x, g_ref[...]).astype(BF16)
    merged = None
    for i in range(N_BRANCH):
        cols = slice(i * D_MODEL, (i + 1) * D_MODEL)
        gate = _sigmoid(_dot(h, wg_ref[:, cols]) + bgate_ref[:, cols])
        rows = slice(i * BRANCH_WIDTH, (i + 1) * BRANCH_WIDTH)
        term = gate * _dot(y_ref[:, rows], p_ref[rows, :])
        merged = term if merged is None else merged + term
    o_ref[...] = x + _dot(merged.astype(BF16), wo_ref[...])


def _row_spec(tm, width):
    return pl.BlockSpec((tm, width), lambda i: (i, 0))


def _const_spec(arr):
    return pl.BlockSpec(arr.shape, lambda i: (0,) * arr.ndim)


def _merge_call(x2, y2, g, wg, bgate, p, wo):
    t, d = x2.shape
    tm = ROW_TILE
    consts = (g, wg, bgate, p, wo)
    return pl.pallas_call(
        _merge_kernel,
        grid=(t // tm,),
        in_specs=[_row_spec(tm, d), _row_spec(tm, y2.shape[1])] + [_const_spec(c) for c in consts],
        out_specs=_row_spec(tm, d),
        out_shape=jax.ShapeDtypeStruct((t, d), F32),
        compiler_params=pltpu.CompilerParams(
            dimension_semantics=("arbitrary",), vmem_limit_bytes=VMEM_LIMIT),
        name="merge",
    )(x2, y2, *consts)


def _swiglu_rows(hb, w1_ref, w3_ref, w2_ref, lead):
    acc = None
    for f in range(D_FF // FF_CHUNK):
        cols = slice(f * FF_CHUNK, (f + 1) * FF_CHUNK)
        up = _dot(hb, w1_ref[lead + (slice(None), cols)])
        gt = _dot(hb, w3_ref[lead + (slice(None), cols)])
        part = _dot((_silu(up) * gt).astype(BF16), w2_ref[lead + (cols, slice(None))])
        acc = part if acc is None else acc + part
    return acc


def _ffn_kernel(x_ref, g_ref, w1_ref, w3_ref, w2_ref, o_ref):
    x = x_ref[...]
    hb = _rms(x, g_ref[...]).astype(BF16)
    o_ref[...] = x + _swiglu_rows(hb, w1_ref, w3_ref, w2_ref, ())


def _ffn_call(x2, g, w1, w3, w2):
    t, d = x2.shape
    tm = ROW_TILE
    consts = (g, w1, w3, w2)
    return pl.pallas_call(
        _ffn_kernel,
        grid=(t // tm,),
        in_specs=[_row_spec(tm, d)] + [_const_spec(c) for c in consts],
        out_specs=_row_spec(tm, d),
        out_shape=jax.ShapeDtypeStruct((t, d), F32),
        compiler_params=pltpu.CompilerParams(
            dimension_semantics=("arbitrary",), vmem_limit_bytes=VMEM_LIMIT),
        name="ffn",
    )(x2, *consts)


ROUTE_IDX0 = N_EXPERTS
ROUTE_W0 = N_EXPERTS + TOP_K
D_TILES = D_MODEL // LANES


def _to_token_tiles(ref, value, lead=()):
    rows = value.shape[0]
    for c in range(D_TILES):
        ref[lead + (pl.ds(c, rows, stride=D_TILES), slice(None))] = value[:, c * LANES:(c + 1) * LANES]


def _from_token_tiles(ref, rows, lead=()):
    return jnp.concatenate(
        [ref[lead + (pl.ds(c, rows, stride=D_TILES), slice(None))] for c in range(D_TILES)], axis=1)


def _router_kernel(x_ref, g_ref, rhi_ref, rlo_ref, h_ref, route_ref):
    x = x_ref[...]
    h = _rms(x, g_ref[...])
    _to_token_tiles(h_ref, h)
    hi, lo = _split(h)
    logits = _dot(hi, rhi_ref[...]) + _dot(hi, rlo_ref[...]) + _dot(lo, rhi_ref[...])
    lane = lax.broadcasted_iota(jnp.int32, logits.shape, 1)
    neg = jnp.float32(-jnp.inf)
    logits = jnp.where(lane < N_EXPERTS, logits, neg)
    m1 = jnp.max(logits, axis=-1, keepdims=True)
    i1 = jnp.min(jnp.where(logits == m1, lane, LANES), axis=-1, keepdims=True)
    rest = jnp.where(lane == i1, neg, logits)
    m2 = jnp.max(rest, axis=-1, keepdims=True)
    i2 = jnp.min(jnp.where(rest == m2, lane, LANES), axis=-1, keepdims=True)
    e2 = jnp.exp(m2 - m1)
    w1 = 1.0 / (1.0 + e2)
    w2 = e2 / (1.0 + e2)
    out = jnp.where(lane == ROUTE_IDX0, i1.astype(F32), 0.0)
    out = jnp.where(lane == ROUTE_IDX0 + 1, i2.astype(F32), out)
    out = jnp.where(lane == ROUTE_W0, w1, out)
    out = jnp.where(lane == ROUTE_W0 + 1, w2, out)
    route_ref[...] = out


def _router_call(x2, g, rhi, rlo):
    t, d = x2.shape
    tm = ROW_TILE
    consts = (g, rhi, rlo)
    return pl.pallas_call(
        _router_kernel,
        grid=(t // tm,),
        in_specs=[_row_spec(tm, d)] + [_const_spec(c) for c in consts],
        out_specs=[_row_spec(tm * D_TILES, LANES), _row_spec(tm, LANES)],
        out_shape=[jax.ShapeDtypeStruct((t * D_TILES, LANES), F32), jax.ShapeDtypeStruct((t, LANES), F32)],
        compiler_params=pltpu.CompilerParams(
            dimension_semantics=("arbitrary",), vmem_limit_bytes=VMEM_LIMIT),
        name="router",
    )(x2, *consts)


GATHER_UNROLL = 8


def _token_tile(ref, r):
    return ref.at[pl.ds(pl.multiple_of(r * D_TILES, D_TILES), D_TILES), :]


def _start_row_gather(idx_ref, src_hbm, dst, sem):
    def body(r, c):
        pltpu.make_async_copy(_token_tile(src_hbm, idx_ref[0, 0, r]), _token_tile(dst, r), sem).start()
        return c

    lax.fori_loop(0, dst.shape[0] // D_TILES, body, 0, unroll=GATHER_UNROLL)


def _wait_row_gather(src_hbm, dst, sem):
    pltpu.make_async_copy(src_hbm.at[pl.ds(0, dst.shape[0]), :], dst, sem).wait()


def _group_kernel(texp_ref, tvalid_ref, rows_ref, rows_next_ref, h_hbm, w1_ref, w3_ref, w2_ref, ys_ref,
                  xbuf, sem):
    i = pl.program_id(0)
    slot = i % 2

    @pl.when(i == 0)
    def _():
        _start_row_gather(rows_ref, h_hbm, xbuf.at[0], sem.at[0])

    @pl.when(i + 1 < pl.num_programs(0))
    def _():
        _start_row_gather(rows_next_ref, h_hbm, xbuf.at[1 - slot], sem.at[1 - slot])

    _wait_row_gather(h_hbm, xbuf.at[slot], sem.at[slot])

    @pl.when(tvalid_ref[i] == 1)
    def _():
        xb = _from_token_tiles(xbuf, GROUP_TILE, (slot,)).astype(BF16)
        _to_token_tiles(ys_ref, _swiglu_rows(xb, w1_ref, w3_ref, w2_ref, (0,)))

    @pl.when(tvalid_ref[i] == 0)
    def _():
        ys_ref[...] = jnp.zeros_like(ys_ref)


def _group_call(tile_expert, tile_valid, row_token, h3, w1, w3, w2):
    n_tiles = tile_expert.shape[0]
    tm = GROUP_TILE
    rows = row_token.reshape(n_tiles, 1, tm)

    def wspec(arr):
        return pl.BlockSpec((1,) + arr.shape[1:], lambda i, te, tv: (te[i], 0, 0))

    grid_spec = pltpu.PrefetchScalarGridSpec(
        num_scalar_prefetch=2,
        grid=(n_tiles,),
        in_specs=[
            pl.BlockSpec((1, 1, tm), lambda i, te, tv: (i, 0, 0), memory_space=pltpu.SMEM),
            pl.BlockSpec((1, 1, tm), lambda i, te, tv: (jnp.minimum(i + 1, n_tiles - 1), 0, 0),
                         memory_space=pltpu.SMEM),
            pl.BlockSpec(memory_space=pl.ANY),
            wspec(w1), wspec(w3), wspec(w2),
        ],
        out_specs=pl.BlockSpec((tm * D_TILES, LANES), lambda i, te, tv: (i, 0)),
        scratch_shapes=[pltpu.VMEM((2, tm * D_TILES, LANES), F32), pltpu.SemaphoreType.DMA((2,))],
    )
    return pl.pallas_call(
        _group_kernel,
        grid_spec=grid_spec,
        out_shape=jax.ShapeDtypeStruct((n_tiles * tm * D_TILES, LANES), F32),
        compiler_params=pltpu.CompilerParams(
            dimension_semantics=("arbitrary",), vmem_limit_bytes=VMEM_LIMIT),
        name="moe_group",
    )(tile_expert, tile_valid, rows, rows, h3, w1, w3, w2)


def _combine_kernel(p0_ref, p1_ref, p0_next_ref, p1_next_ref, ys_hbm, x_ref, route_ref, g_ref, o_ref,
                    buf, sem, *, apply_final_norm):
    i = pl.program_id(0)
    slot = i % 2

    def start(refs, s):
        for j in range(TOP_K):
            _start_row_gather(refs[j], ys_hbm, buf.at[s, j], sem.at[s, j])

    @pl.when(i == 0)
    def _():
        start((p0_ref, p1_ref), 0)

    @pl.when(i + 1 < pl.num_programs(0))
    def _():
        start((p0_next_ref, p1_next_ref), 1 - slot)

    for j in range(TOP_K):
        _wait_row_gather(ys_hbm, buf.at[slot, j], sem.at[slot, j])
    route = route_ref[...]
    y = x_ref[...]
    for j in range(TOP_K):
        y = y + route[:, ROUTE_W0 + j:ROUTE_W0 + j + 1] * _from_token_tiles(buf, ROW_TILE, (slot, j))
    o_ref[...] = _rms(y, g_ref[...]) if apply_final_norm else y


def _combine_call(pos0, pos1, ys, x2, route, g, apply_final_norm):
    t, d = x2.shape
    tm = ROW_TILE
    n = t // tm

    def ispec(shift):
        return pl.BlockSpec((1, 1, tm), lambda i: (jnp.minimum(i + shift, n - 1), 0, 0),
                            memory_space=pltpu.SMEM)

    p0 = pos0.reshape(n, 1, tm)
    p1 = pos1.reshape(n, 1, tm)
    return pl.pallas_call(
        functools.partial(_combine_kernel, apply_final_norm=apply_final_norm),
        grid=(n,),
        in_specs=[ispec(0), ispec(0), ispec(1), ispec(1), pl.BlockSpec(memory_space=pl.ANY),
                  _row_spec(tm, d), _row_spec(tm, LANES), _const_spec(g)],
        out_specs=_row_spec(tm, d),
        out_shape=jax.ShapeDtypeStruct((t, d), F32),
        scratch_shapes=[pltpu.VMEM((2, TOP_K, tm * D_TILES, LANES), F32),
                        pltpu.SemaphoreType.DMA((2, TOP_K))],
        compiler_params=pltpu.CompilerParams(
            dimension_semantics=("arbitrary",), vmem_limit_bytes=VMEM_LIMIT),
        name="moe_combine",
    )(p0, p1, p0, p1, ys, x2, route, g)


def _final_norm_kernel(x_ref, g_ref, o_ref):
    o_ref[...] = _rms(x_ref[...], g_ref[...])


def _final_norm_call(x2, g):
    t, d = x2.shape
    tm = ROW_TILE
    return pl.pallas_call(
        _final_norm_kernel,
        grid=(t // tm,),
        in_specs=[_row_spec(tm, d), _const_spec(g)],
        out_specs=_row_spec(tm, d),
        out_shape=jax.ShapeDtypeStruct((t, d), F32),
        name="final_norm",
    )(x2, g)


def _routing_plan(route, n_tiles):
    t = route.shape[0]
    tm = GROUP_TILE
    experts = route[:, ROUTE_IDX0:ROUTE_IDX0 + TOP_K].astype(jnp.int32).reshape(t * TOP_K)
    onehot = (experts[:, None] == jnp.arange(N_EXPERTS, dtype=jnp.int32)[None, :]).astype(jnp.int32)
    rank = jnp.sum((jnp.cumsum(onehot, axis=0) - onehot) * onehot, axis=1)
    counts = jnp.sum(onehot, axis=0)
    tiles_per = (counts + tm - 1) // tm
    tile_end = jnp.cumsum(tiles_per)
    tile_start = tile_end - tiles_per
    pos = tile_start[experts] * tm + rank
    tile_ids = jnp.arange(n_tiles, dtype=jnp.int32)
    tile_expert = jnp.minimum(
        jnp.sum((tile_ids[:, None] >= tile_end[None, :]).astype(jnp.int32), axis=1), N_EXPERTS - 1)
    tile_valid = (tile_ids < tile_end[-1]).astype(jnp.int32)
    order = jnp.argsort(experts, stable=True).astype(jnp.int32)
    row_expert = jnp.repeat(tile_expert, tm)
    row_rank = jnp.arange(n_tiles * tm, dtype=jnp.int32) - tile_start[row_expert] * tm
    row_used = (row_rank < counts[row_expert]) & (jnp.repeat(tile_valid, tm) == 1)
    src = jnp.clip((jnp.cumsum(counts) - counts)[row_expert] + row_rank, 0, t * TOP_K - 1)
    row_token = jnp.where(row_used, order[src] // TOP_K, 0)
    last_expert = tile_expert[jnp.maximum(tile_end[-1] - 1, 0)]
    tile_expert = jnp.where(tile_valid == 1, tile_expert, last_expert)
    pos2 = pos.reshape(t, TOP_K)
    return tile_expert, tile_valid, row_token, pos2[:, 0], pos2[:, 1]


def _pack_mixer_weights(w_in):
    glr = jnp.pad(w_in[:, OFF_GLR:GATE_OFF], ((0, 0), (0, LANES - GLA_RANK)))
    return jnp.concatenate([w_in[:, :OFF_GLR], glr], axis=1).astype(BF16)


def _row(v):
    return v.reshape(1, -1).astype(F32)


def kernel(x, norm_mix, w_in, b_gate, sgu_w, sgu_b, sgu_norm, conv_w, conv_b, pool_w, pool_scale, gla_wg2, gla_bg, gla_norm, branch_proj, w_out, norm_ffn, ffn_w1, ffn_w3, ffn_w2, moe_router, moe_w1, moe_w3, moe_w2, final_norm):
    b, s, d = x.shape
    t = b * s
    depth = norm_mix.shape[0]
    causal = jnp.tril(jnp.ones((SGU_CHUNK, SGU_CHUNK), dtype=bool))
    out = None
    for l in range(depth):
        w_mix = _pack_mixer_weights(w_in[l])
        sguw = jnp.concatenate([jnp.where(causal, sgu_w[l, hh], 0.0) for hh in range(SGU_HEADS)],
                               axis=0).astype(BF16)
        sgub = jnp.repeat(sgu_b[l].T, SGU_HEAD_DIM, axis=1).astype(F32)
        pw = jax.scipy.linalg.block_diag(*[pool_w[l, gi] for gi in range(len(POOL_WINDOWS))]).astype(BF16)
        wg2 = jnp.pad(gla_wg2[l], ((0, LANES - GLA_RANK), (0, 0))).astype(BF16)
        gn = jnp.tile(gla_norm[l], GLA_HEADS)
        y = _mixer_call(x, _row(norm_mix[l]), w_mix, sguw, sgub, _row(sgu_norm[l]),
                        conv_w[l].astype(F32), _row(conv_b[l]), pw, _row(pool_scale[l]),
                        wg2, _row(gla_bg[l]), _row(gn))
        x2 = _merge_call(x.reshape(t, d), y.reshape(t, N_BRANCH * BRANCH_WIDTH), _row(norm_mix[l]),
                         w_in[l][:, GATE_OFF:].astype(BF16), _row(b_gate[l]),
                         branch_proj[l].reshape(N_BRANCH * BRANCH_WIDTH, d).astype(BF16),
                         w_out[l].astype(BF16))
        i = l // 2
        if l % 2 == 0:
            x2 = _ffn_call(x2, _row(norm_ffn[l]), ffn_w1[i].astype(BF16), ffn_w3[i].astype(BF16),
                           ffn_w2[i].astype(BF16))
            if l == depth - 1:
                out = _final_norm_call(x2, _row(final_norm))
        else:
            router = jnp.pad(moe_router[i], ((0, 0), (0, LANES - N_EXPERTS)))
            rhi = router.astype(BF16)
            rlo = (router - rhi.astype(F32)).astype(BF16)
            h2, route = _router_call(x2, _row(norm_ffn[l]), rhi, rlo)
            n_tiles = (t * TOP_K) // GROUP_TILE + N_EXPERTS
            tile_expert, tile_valid, row_token, pos0, pos1 = _routing_plan(route, n_tiles)
            ys = _group_call(tile_expert, tile_valid, row_token, h2,
                             moe_w1[i].astype(BF16), moe_w3[i].astype(BF16), moe_w2[i].astype(BF16))
            x2 = _combine_call(pos0, pos1, ys, x2, route, _row(final_norm), l == depth - 1)
            if l == depth - 1:
                out = x2
        x = x2.reshape(b, s, d)
    return out.reshape(b, s, d)
```

```python
import functools

import numpy as np
import jax
import jax.numpy as jnp
from jax import lax
from jax.experimental import pallas as pl
from jax.experimental.pallas import tpu as pltpu

F32 = jnp.float32
BF16 = jnp.bfloat16

D_MODEL = 1024
N_BRANCH = 4
BRANCH_WIDTH = 256
SGU_HEADS = 4
SGU_HEAD_DIM = BRANCH_WIDTH // SGU_HEADS
SGU_CHUNK = 128
CONV_WIDTH = 3
POOL_WINDOWS = (2, 4, 8, 16)
POOL_GROUP_DIM = BRANCH_WIDTH // len(POOL_WINDOWS)
POOL_HISTORY = 16
POOL_PAD = 8
assert POOL_WINDOWS == (2, 4, 8, 16)
CONV_HISTORY = 8
GLA_HEADS = 4
GLA_DK = 32
GLA_DV = BRANCH_WIDTH // GLA_HEADS
GLA_HK = GLA_HEADS * GLA_DK
GLA_RANK = 16
GLA_TAU = 16.0
GLA_CHUNK = 64
D_FF = 2816
N_EXPERTS = 8
TOP_K = 2
EPS = 1e-6

A_COLS = 2 * BRANCH_WIDTH
B_COLS = 3 * BRANCH_WIDTH
C_COLS = BRANCH_WIDTH
QKVR_COLS = 2 * GLA_HK + 2 * BRANCH_WIDTH
LANES = 128
OFF_A = 0
OFF_B = OFF_A + A_COLS
OFF_C = OFF_B + B_COLS
OFF_D = OFF_C + C_COLS
OFF_GLR = OFF_D + QKVR_COLS
MIX_COLS = OFF_GLR + LANES
GATE_OFF = OFF_GLR + GLA_RANK

MIX_TS = 512
ROW_TILE = 512
GROUP_TILE = 512
FF_CHUNK = 256
VMEM_LIMIT = 56 * 1024 * 1024


def _dot(a, b):
    return jnp.dot(a, b, preferred_element_type=F32)


def _dot_nt(a, b):
    return lax.dot_general(a, b, (((1,), (1,)), ((), ())), preferred_element_type=F32)


def _dot_tn(a, b):
    return lax.dot_general(a, b, (((0,), (0,)), ((), ())), preferred_element_type=F32)


def _split(x):
    hi = x.astype(BF16)
    lo = (x - hi.astype(F32)).astype(BF16)
    return hi, lo


def _rms(x, g):
    ms = jnp.mean(x * x, axis=-1, keepdims=True)
    return x * lax.rsqrt(ms + EPS) * g


def _sigmoid(x):
    return 1.0 / (1.0 + jnp.exp(-x))


def _silu(x):
    return x * _sigmoid(x)


def _gelu_tanh(x):
    c = np.float32(np.sqrt(2.0 / np.pi))
    return x * (0.5 * (1.0 + jnp.tanh(c * (x + 0.044715 * (x * x * x)))))


def _group_mean_matrix(n, group):
    r = lax.broadcasted_iota(jnp.int32, (n, n), 0) // group
    c = lax.broadcasted_iota(jnp.int32, (n, n), 1) // group
    return jnp.where(r == c, 1.0 / group, 0.0).astype(BF16)


def _group_rms(x, bd, g):
    hi, lo = _split(x * x)
    ms = _dot(hi, bd) + _dot(lo, bd)
    return x * lax.rsqrt(ms + EPS) * g


def _mixer_kernel(x_ref, g_ref, w_ref, sguw_ref, sgub_ref, sgun_ref, cw_ref, cb_ref, pw_ref, ps_ref,
                  wg2_ref, bg_ref, gn_ref, y_ref,
                  state_ref, ybuf_ref, zbuf_ref):
    si = pl.program_id(1)
    ts = x_ref.shape[1]
    x = x_ref[0]
    h = _rms(x, g_ref[...]).astype(BF16)

    @pl.when(si == 0)
    def _():
        state_ref[...] = jnp.zeros_like(state_ref)
        ybuf_ref[0:CONV_HISTORY, :] = jnp.zeros((CONV_HISTORY, BRANCH_WIDTH), F32)
        zbuf_ref[:, 0:POOL_PAD + POOL_HISTORY, :] = jnp.zeros((3, POOL_PAD + POOL_HISTORY, BRANCH_WIDTH), F32)

    bd64 = _group_mean_matrix(BRANCH_WIDTH, SGU_HEAD_DIM)
    lane = lax.broadcasted_iota(jnp.int32, (1, BRANCH_WIDTH), 1)

    z = _gelu_tanh(_dot(h, w_ref[:, OFF_A:OFF_A + A_COLS]))
    u = z[:, :BRANCH_WIDTH]
    vn = _group_rms(z[:, BRANCH_WIDTH:], bd64, sgun_ref[...])
    lane_head = lane // SGU_HEAD_DIM
    vnb = vn.astype(BF16)
    for c in range(ts // SGU_CHUNK):
        rows = slice(c * SGU_CHUNK, (c + 1) * SGU_CHUNK)
        stacked = _dot(sguw_ref[...], vnb[rows])
        mixed = stacked[(SGU_HEADS - 1) * SGU_CHUNK:]
        for hh in range(SGU_HEADS - 2, -1, -1):
            mixed = jnp.where(lane_head == hh, stacked[hh * SGU_CHUNK:(hh + 1) * SGU_CHUNK], mixed)
        y_ref[0, rows, 0:BRANCH_WIDTH] = (u[rows] * (mixed + sgub_ref[...])).astype(BF16)

    zb = _dot(h, w_ref[:, OFF_B:OFF_B + B_COLS])
    gate_b = zb[:, BRANCH_WIDTH:2 * BRANCH_WIDTH]
    yv = zb[:, 2 * BRANCH_WIDTH:] * zb[:, :BRANCH_WIDTH]
    ybuf_ref[CONV_HISTORY:CONV_HISTORY + ts, :] = yv
    conv = cb_ref[...] + yv * cw_ref[CONV_WIDTH - 1:CONV_WIDTH, :]
    for i in range(CONV_WIDTH - 1):
        back = CONV_WIDTH - 1 - i
        conv = conv + ybuf_ref[CONV_HISTORY - back:CONV_HISTORY - back + ts, :] * cw_ref[i:i + 1, :]
    ybuf_ref[0:CONV_HISTORY, :] = ybuf_ref[ts:ts + CONV_HISTORY, :]
    y_ref[0, :, BRANCH_WIDTH:2 * BRANCH_WIDTH] = (gate_b * conv).astype(BF16)

    zc = _dot(h, w_ref[:, OFF_C:OFF_C + C_COLS])
    ext = POOL_HISTORY + ts
    p0 = POOL_PAD
    zbuf_ref[0, p0 + POOL_HISTORY:p0 + ext, :] = zc
    s2 = zbuf_ref[0, p0:p0 + ext, :] + zbuf_ref[0, p0 - 1:p0 - 1 + ext, :]
    zbuf_ref[1, p0:p0 + ext, :] = s2
    s4 = s2 + zbuf_ref[1, p0 - 2:p0 - 2 + ext, :]
    zbuf_ref[2, p0:p0 + ext, :] = s4
    s8 = s4 + zbuf_ref[2, p0 - 4:p0 - 4 + ext, :]
    sums = {2: s2[POOL_HISTORY:], 4: s4[POOL_HISTORY:], 8: s8[POOL_HISTORY:],
            16: s8[POOL_HISTORY:] + s8[POOL_HISTORY - 8:POOL_HISTORY - 8 + ts]}
    zbuf_ref[0, p0:p0 + POOL_HISTORY, :] = zbuf_ref[0, p0 + ts:p0 + ext, :]
    lane_group = lane // POOL_GROUP_DIM
    wsum = sums[POOL_WINDOWS[-1]]
    win = jnp.full((1, BRANCH_WIDTH), POOL_WINDOWS[-1], jnp.int32)
    for gi in range(len(POOL_WINDOWS) - 2, -1, -1):
        wsum = jnp.where(lane_group == gi, sums[POOL_WINDOWS[gi]], wsum)
        win = jnp.where(lane_group == gi, POOL_WINDOWS[gi], win)
    tpos = si * ts + lax.broadcasted_iota(jnp.int32, (ts, 1), 0)
    count = jnp.minimum(tpos + 1, win).astype(F32)
    pooled = wsum / count - zc
    y_ref[0, :, 2 * BRANCH_WIDTH:3 * BRANCH_WIDTH] = (
        _dot(pooled.astype(BF16), pw_ref[...]) * ps_ref[...]).astype(BF16)

    zd = _dot(h, w_ref[:, OFF_D:OFF_D + QKVR_COLS + LANES])
    q = zd[:, 0:GLA_HK]
    k = zd[:, GLA_HK:2 * GLA_HK]
    v = zd[:, 2 * GLA_HK:2 * GLA_HK + BRANCH_WIDTH]
    r = zd[:, 2 * GLA_HK + BRANCH_WIDTH:QKVR_COLS]
    glr = zd[:, QKVR_COLS:]
    a = _dot(glr.astype(BF16), wg2_ref[...]) + bg_ref[...]
    log_g = (jnp.minimum(a, 0.0) - jnp.log(1.0 + jnp.exp(-jnp.abs(a)))) * (1.0 / GLA_TAU)
    blk = 4 * GLA_CHUNK
    rr = lax.broadcasted_iota(jnp.int32, (blk, blk), 0)
    cc = lax.broadcasted_iota(jnp.int32, (blk, blk), 1)
    same = (rr // GLA_CHUNK) == (cc // GLA_CHUNK)
    tri = jnp.where(same & (cc <= rr), 1.0, 0.0).astype(BF16)
    ones = jnp.where(same, 1.0, 0.0).astype(BF16)
    cums, lasts = [], []
    for b in range(ts // blk):
        hi, lo = _split(log_g[b * blk:(b + 1) * blk])
        cums.append(_dot(tri, hi) + _dot(tri, lo))
        lasts.append(_dot(ones, hi) + _dot(ones, lo))
    cum = jnp.concatenate(cums, axis=0)
    last = jnp.concatenate(lasts, axis=0)
    qd_all = (q * (GLA_DK ** -0.5) * jnp.exp(cum)).astype(BF16)
    ki_all = (k * jnp.exp(-cum)).astype(BF16)
    ke_all = (k * jnp.exp(last - cum)).astype(BF16)
    v_all = v.astype(BF16)
    decay_all = jnp.exp(last)

    nrow = GLA_HEADS * GLA_CHUNK
    row_head = lax.broadcasted_iota(jnp.int32, (nrow, 1), 0) // GLA_CHUNK
    mask_k = jnp.where(row_head == lax.broadcasted_iota(jnp.int32, (nrow, GLA_HK), 1) // GLA_DK,
                       1.0, 0.0).astype(BF16)
    mask_v = jnp.where(row_head == lax.broadcasted_iota(jnp.int32, (nrow, BRANCH_WIDTH), 1) // GLA_DV,
                       1.0, 0.0).astype(BF16)
    causal = (lax.broadcasted_iota(jnp.int32, (GLA_CHUNK, nrow), 1) % GLA_CHUNK
              <= lax.broadcasted_iota(jnp.int32, (GLA_CHUNK, nrow), 0))
    mask_s = (lax.broadcasted_iota(jnp.int32, (BRANCH_WIDTH, GLA_HK), 0) // GLA_DV
              == lax.broadcasted_iota(jnp.int32, (BRANCH_WIDTH, GLA_HK), 1) // GLA_DK)

    state = state_ref[...]
    outs = []
    for n in range(ts // GLA_CHUNK):
        rows = slice(n * GLA_CHUNK, (n + 1) * GLA_CHUNK)
        qd = qd_all[rows]
        vc = v_all[rows]
        kbd = jnp.concatenate([ki_all[rows]] * GLA_HEADS, axis=0) * mask_k
        vbd = jnp.concatenate([vc] * GLA_HEADS, axis=0) * mask_v
        scores = jnp.where(causal, _dot_nt(qd, kbd), 0.0)
        outs.append(_dot(scores.astype(BF16), vbd) + _dot_nt(qd, state.astype(BF16)))
        kv = _dot_tn(vc, ke_all[rows])
        state = state * decay_all[n * GLA_CHUNK:n * GLA_CHUNK + 1, :] + jnp.where(mask_s, kv, 0.0)
    state_ref[...] = state
    on = _group_rms(jnp.concatenate(outs, axis=0), bd64, gn_ref[...])
    y_ref[0, :, 3 * BRANCH_WIDTH:] = (_silu(r) * on).astype(BF16)


def _mixer_call(x, g, w_mix, sguw, sgub, sgun, cw, cb, pw, ps, wg2, bg, gn):
    b, s, d = x.shape
    ts = MIX_TS

    def const(arr):
        return pl.BlockSpec(arr.shape, lambda bi, si: (0,) * arr.ndim)

    consts = (g, w_mix, sguw, sgub, sgun, cw, cb, pw, ps, wg2, bg, gn)
    return pl.pallas_call(
        _mixer_kernel,
        grid=(b, s // ts),
        in_specs=[pl.BlockSpec((1, ts, d), lambda bi, si: (bi, si, 0))] + [const(c) for c in consts],
        out_specs=pl.BlockSpec((1, ts, N_BRANCH * BRANCH_WIDTH), lambda bi, si: (bi, si, 0)),
        out_shape=jax.ShapeDtypeStruct((b, s, N_BRANCH * BRANCH_WIDTH), BF16),
        scratch_shapes=[
            pltpu.VMEM((BRANCH_WIDTH, GLA_HK), F32),
            pltpu.VMEM((CONV_HISTORY + ts, BRANCH_WIDTH), F32),
            pltpu.VMEM((3, POOL_PAD + POOL_HISTORY + ts, BRANCH_WIDTH), F32),
        ],
        compiler_params=pltpu.CompilerParams(
            dimension_semantics=("arbitrary", "arbitrary"), vmem_limit_bytes=VMEM_LIMIT),
        name="mixer",
    )(x, *consts)


def _merge_kernel(x_ref, y_ref, g_ref, wg_ref, bgate_ref, p_ref, wo_ref, o_ref):
    x = x_ref[...]
    h = _rms(x, g_ref[...]).astype(BF16)
    merged = None
    for i in range(N_BRANCH):
        cols = slice(i * D_MODEL, (i + 1) * D_MODEL)
        gate = _sigmoid(_dot(h, wg_ref[:, cols]) + bgate_ref[:, cols])
        rows = slice(i * BRANCH_WIDTH, (i + 1) * BRANCH_WIDTH)
        term = gate * _dot(y_ref[:, rows], p_ref[rows, :])
        merged = term if merged is None else merged + term
    o_ref[...] = x + _dot(merged.astype(BF16), wo_ref[...])


def _row_spec(tm, width):
    return pl.BlockSpec((tm, width), lambda i: (i, 0))


def _const_spec(arr):
    return pl.BlockSpec(arr.shape, lambda i: (0,) * arr.ndim)


def _merge_call(x2, y2, g, wg, bgate, p, wo):
    t, d = x2.shape
    tm = ROW_TILE
    consts = (g, wg, bgate, p, wo)
    return pl.pallas_call(
        _merge_kernel,
        grid=(t // tm,),
        in_specs=[_row_spec(tm, d), _row_spec(tm, y2.shape[1])] + [_const_spec(c) for c in consts],
        out_specs=_row_spec(tm, d),
        out_shape=jax.ShapeDtypeStruct((t, d), F32),
        compiler_params=pltpu.CompilerParams(
            dimension_semantics=("arbitrary",), vmem_limit_bytes=VMEM_LIMIT),
        name="merge",
    )(x2, y2, *consts)


FF_CHUNKS = D_FF // FF_CHUNK


def _swiglu_rows(hb, w1_ref, w3_ref, w2_ref, lead, between=None):
    acc = None
    for f in range(FF_CHUNKS):
        cols = slice(f * FF_CHUNK, (f + 1) * FF_CHUNK)
        up = _dot(hb, w1_ref[lead + (slice(None), cols)])
        gt = _dot(hb, w3_ref[lead + (slice(None), cols)])
        part = _dot((_silu(up) * gt).astype(BF16), w2_ref[lead + (cols, slice(None))])
        acc = part if acc is None else acc + part
        if between is not None:
            between(f)
    return acc


def _ffn_kernel(x_ref, g_ref, w1_ref, w3_ref, w2_ref, o_ref):
    x = x_ref[...]
    hb = _rms(x, g_ref[...]).astype(BF16)
    o_ref[...] = x + _swiglu_rows(hb, w1_ref, w3_ref, w2_ref, ())


def _ffn_call(x2, g, w1, w3, w2):
    t, d = x2.shape
    tm = ROW_TILE
    consts = (g, w1, w3, w2)
    return pl.pallas_call(
        _ffn_kernel,
        grid=(t // tm,),
        in_specs=[_row_spec(tm, d)] + [_const_spec(c) for c in consts],
        out_specs=_row_spec(tm, d),
        out_shape=jax.ShapeDtypeStruct((t, d), F32),
        compiler_params=pltpu.CompilerParams(
            dimension_semantics=("arbitrary",), vmem_limit_bytes=VMEM_LIMIT),
        name="ffn",
    )(x2, *consts)


ROUTE_IDX0 = N_EXPERTS
ROUTE_W0 = N_EXPERTS + TOP_K
D_TILES = D_MODEL // LANES


def _to_token_tiles(ref, value, lead=()):
    rows = value.shape[0]
    for c in range(D_TILES):
        ref[lead + (pl.ds(c, rows, stride=D_TILES), slice(None))] = value[:, c * LANES:(c + 1) * LANES]


def _from_token_tiles(ref, rows, lead=()):
    return jnp.concatenate(
        [ref[lead + (pl.ds(c, rows, stride=D_TILES), slice(None))] for c in range(D_TILES)], axis=1)


def _route_rows(x, g_ref, rhi_ref, rlo_ref, h_ref, route_ref):
    h = _rms(x, g_ref[...])
    _to_token_tiles(h_ref, h)
    hi, lo = _split(h)
    logits = _dot(hi, rhi_ref[...]) + _dot(hi, rlo_ref[...]) + _dot(lo, rhi_ref[...])
    lane = lax.broadcasted_iota(jnp.int32, logits.shape, 1)
    neg = jnp.float32(-jnp.inf)
    logits = jnp.where(lane < N_EXPERTS, logits, neg)
    m1 = jnp.max(logits, axis=-1, keepdims=True)
    i1 = jnp.min(jnp.where(logits == m1, lane, LANES), axis=-1, keepdims=True)
    rest = jnp.where(lane == i1, neg, logits)
    m2 = jnp.max(rest, axis=-1, keepdims=True)
    i2 = jnp.min(jnp.where(rest == m2, lane, LANES), axis=-1, keepdims=True)
    e2 = jnp.exp(m2 - m1)
    w1 = 1.0 / (1.0 + e2)
    w2 = e2 / (1.0 + e2)
    out = jnp.where(lane == ROUTE_IDX0, i1.astype(F32), 0.0)
    out = jnp.where(lane == ROUTE_IDX0 + 1, i2.astype(F32), out)
    out = jnp.where(lane == ROUTE_W0, w1, out)
    out = jnp.where(lane == ROUTE_W0 + 1, w2, out)
    route_ref[...] = out


def _router_kernel(x_ref, g_ref, rhi_ref, rlo_ref, h_ref, route_ref):
    _route_rows(x_ref[...], g_ref, rhi_ref, rlo_ref, h_ref, route_ref)


def _router_call(x2, g, rhi, rlo):
    t, d = x2.shape
    tm = ROW_TILE
    consts = (g, rhi, rlo)
    return pl.pallas_call(
        _router_kernel,
        grid=(t // tm,),
        in_specs=[_row_spec(tm, d)] + [_const_spec(c) for c in consts],
        out_specs=[_row_spec(tm * D_TILES, LANES), _row_spec(tm, LANES)],
        out_shape=[jax.ShapeDtypeStruct((t * D_TILES, LANES), F32), jax.ShapeDtypeStruct((t, LANES), F32)],
        compiler_params=pltpu.CompilerParams(
            dimension_semantics=("arbitrary",), vmem_limit_bytes=VMEM_LIMIT),
        name="router",
    )(x2, *consts)


GATHER_UNROLL = 8


def _token_tile(ref, r):
    return ref.at[pl.ds(pl.multiple_of(r * D_TILES, D_TILES), D_TILES), :]


def _start_row_gather(idx_ref, src_hbm, dst, sem):
    def body(r, c):
        pltpu.make_async_copy(_token_tile(src_hbm, idx_ref[0, 0, r]), _token_tile(dst, r), sem).start()
        return c

    lax.fori_loop(0, dst.shape[0] // D_TILES, body, 0, unroll=GATHER_UNROLL)


def _wait_row_gather(src_hbm, dst, sem):
    pltpu.make_async_copy(src_hbm.at[pl.ds(0, dst.shape[0]), :], dst, sem).wait()


def _group_kernel(texp_ref, tvalid_ref, rows_ref, rows_next_ref, h_hbm, w1_ref, w3_ref, w2_ref, ys_ref,
                  xbuf, sem):
    i = pl.program_id(0)
    slot = i % 2

    nxt = 1 - slot
    per_chunk = pl.cdiv(GROUP_TILE, FF_CHUNKS)

    @pl.when(i == 0)
    def _():
        _start_row_gather(rows_ref, h_hbm, xbuf.at[0], sem.at[0])

    _wait_row_gather(h_hbm, xbuf.at[slot], sem.at[slot])

    def start_next_rows(f):
        for r in range(f * per_chunk, min((f + 1) * per_chunk, GROUP_TILE)):
            pltpu.make_async_copy(_token_tile(h_hbm, rows_next_ref[0, 0, r]), _token_tile(xbuf.at[nxt], r),
                                  sem.at[nxt]).start()

    @pl.when(tvalid_ref[i] == 1)
    def _():
        xb = _from_token_tiles(xbuf, GROUP_TILE, (slot,)).astype(BF16)
        _to_token_tiles(ys_ref, _swiglu_rows(xb, w1_ref, w3_ref, w2_ref, (0,), between=start_next_rows))

    @pl.when(tvalid_ref[i] == 0)
    def _():
        _start_row_gather(rows_next_ref, h_hbm, xbuf.at[nxt], sem.at[nxt])
        ys_ref[...] = jnp.zeros_like(ys_ref)

    @pl.when(i + 1 == pl.num_programs(0))
    def _():
        _wait_row_gather(h_hbm, xbuf.at[nxt], sem.at[nxt])


def _group_call(tile_expert, tile_valid, row_token, h3, w1, w3, w2):
    n_tiles = tile_expert.shape[0]
    tm = GROUP_TILE
    rows = row_token.reshape(n_tiles, 1, tm)

    def wspec(arr):
        return pl.BlockSpec((1,) + arr.shape[1:], lambda i, te, tv: (te[i], 0, 0))

    grid_spec = pltpu.PrefetchScalarGridSpec(
        num_scalar_prefetch=2,
        grid=(n_tiles,),
        in_specs=[
            pl.BlockSpec((1, 1, tm), lambda i, te, tv: (i, 0, 0), memory_space=pltpu.SMEM),
            pl.BlockSpec((1, 1, tm), lambda i, te, tv: (jnp.minimum(i + 1, n_tiles - 1), 0, 0),
                         memory_space=pltpu.SMEM),
            pl.BlockSpec(memory_space=pl.ANY),
            wspec(w1), wspec(w3), wspec(w2),
        ],
        out_specs=pl.BlockSpec((tm * D_TILES, LANES), lambda i, te, tv: (i, 0)),
        scratch_shapes=[pltpu.VMEM((2, tm * D_TILES, LANES), F32), pltpu.SemaphoreType.DMA((2,))],
    )
    return pl.pallas_call(
        _group_kernel,
        grid_spec=grid_spec,
        out_shape=jax.ShapeDtypeStruct((n_tiles * tm * D_TILES, LANES), F32),
        compiler_params=pltpu.CompilerParams(
            dimension_semantics=("arbitrary",), vmem_limit_bytes=VMEM_LIMIT),
        name="moe_group",
    )(tile_expert, tile_valid, rows, rows, h3, w1, w3, w2)


def _combine_kernel(p0_ref, p1_ref, p0_next_ref, p1_next_ref, ys_hbm, x_ref, route_ref, g_ref, o_ref,
                    buf, sem, *, apply_final_norm):
    i = pl.program_id(0)
    slot = i % 2

    def start(refs, s):
        for j in range(TOP_K):
            _start_row_gather(refs[j], ys_hbm, buf.at[s, j], sem.at[s, j])

    @pl.when(i == 0)
    def _():
        start((p0_ref, p1_ref), 0)

    @pl.when(i + 1 < pl.num_programs(0))
    def _():
        start((p0_next_ref, p1_next_ref), 1 - slot)

    for j in range(TOP_K):
        _wait_row_gather(ys_hbm, buf.at[slot, j], sem.at[slot, j])
    route = route_ref[...]
    y = x_ref[...]
    for j in range(TOP_K):
        y = y + route[:, ROUTE_W0 + j:ROUTE_W0 + j + 1] * _from_token_tiles(buf, ROW_TILE, (slot, j))
    o_ref[...] = _rms(y, g_ref[...]) if apply_final_norm else y


def _combine_call(pos0, pos1, ys, x2, route, g, apply_final_norm):
    t, d = x2.shape
    tm = ROW_TILE
    n = t // tm

    def ispec(shift):
        return pl.BlockSpec((1, 1, tm), lambda i: (jnp.minimum(i + shift, n - 1), 0, 0),
                            memory_space=pltpu.SMEM)

    p0 = pos0.reshape(n, 1, tm)
    p1 = pos1.reshape(n, 1, tm)
    return pl.pallas_call(
        functools.partial(_combine_kernel, apply_final_norm=apply_final_norm),
        grid=(n,),
        in_specs=[ispec(0), ispec(0), ispec(1), ispec(1), pl.BlockSpec(memory_space=pl.ANY),
                  _row_spec(tm, d), _row_spec(tm, LANES), _const_spec(g)],
        out_specs=_row_spec(tm, d),
        out_shape=jax.ShapeDtypeStruct((t, d), F32),
        scratch_shapes=[pltpu.VMEM((2, TOP_K, tm * D_TILES, LANES), F32),
                        pltpu.SemaphoreType.DMA((2, TOP_K))],
        compiler_params=pltpu.CompilerParams(
            dimension_semantics=("arbitrary",), vmem_limit_bytes=VMEM_LIMIT),
        name="moe_combine",
    )(p0, p1, p0, p1, ys, x2, route, g)


def _final_norm_kernel(x_ref, g_ref, o_ref):
    o_ref[...] = _rms(x_ref[...], g_ref[...])


def _final_norm_call(x2, g):
    t, d = x2.shape
    tm = ROW_TILE
    return pl.pallas_call(
        _final_norm_kernel,
        grid=(t // tm,),
        in_specs=[_row_spec(tm, d), _const_spec(g)],
        out_specs=_row_spec(tm, d),
        out_shape=jax.ShapeDtypeStruct((t, d), F32),
        name="final_norm",
    )(x2, g)


def _routing_plan(route, n_tiles):
    t = route.shape[0]
    tm = GROUP_TILE
    experts = route[:, ROUTE_IDX0:ROUTE_IDX0 + TOP_K].astype(jnp.int32).reshape(t * TOP_K)
    onehot = (experts[:, None] == jnp.arange(N_EXPERTS, dtype=jnp.int32)[None, :]).astype(jnp.int32)
    rank = jnp.sum((jnp.cumsum(onehot, axis=0) - onehot) * onehot, axis=1)
    counts = jnp.sum(onehot, axis=0)
    tiles_per = (counts + tm - 1) // tm
    tile_end = jnp.cumsum(tiles_per)
    tile_start = tile_end - tiles_per
    pos = tile_start[experts] * tm + rank
    tile_ids = jnp.arange(n_tiles, dtype=jnp.int32)
    tile_expert = jnp.minimum(
        jnp.sum((tile_ids[:, None] >= tile_end[None, :]).astype(jnp.int32), axis=1), N_EXPERTS - 1)
    tile_valid = (tile_ids < tile_end[-1]).astype(jnp.int32)
    order = jnp.argsort(experts, stable=True).astype(jnp.int32)
    row_expert = jnp.repeat(tile_expert, tm)
    row_rank = jnp.arange(n_tiles * tm, dtype=jnp.int32) - tile_start[row_expert] * tm
    row_used = (row_rank < counts[row_expert]) & (jnp.repeat(tile_valid, tm) == 1)
    src = jnp.clip((jnp.cumsum(counts) - counts)[row_expert] + row_rank, 0, t * TOP_K - 1)
    row_token = jnp.where(row_used, order[src] // TOP_K, 0)
    last_expert = tile_expert[jnp.maximum(tile_end[-1] - 1, 0)]
    tile_expert = jnp.where(tile_valid == 1, tile_expert, last_expert)
    pos2 = pos.reshape(t, TOP_K)
    return tile_expert, tile_valid, row_token, pos2[:, 0], pos2[:, 1]


def _pack_mixer_weights(w_in, l):
    glr = jnp.pad(w_in[l, :, OFF_GLR:GATE_OFF], ((0, 0), (0, LANES - GLA_RANK)))
    return jnp.concatenate([w_in[l, :, :OFF_GLR], glr], axis=1).astype(BF16)


def _row(v):
    return v.reshape(1, -1).astype(F32)


def kernel(x, norm_mix, w_in, b_gate, sgu_w, sgu_b, sgu_norm, conv_w, conv_b, pool_w, pool_scale, gla_wg2, gla_bg, gla_norm, branch_proj, w_out, norm_ffn, ffn_w1, ffn_w3, ffn_w2, moe_router, moe_w1, moe_w3, moe_w2, final_norm):
    b, s, d = x.shape
    t = b * s
    depth = norm_mix.shape[0]
    causal = jnp.tril(jnp.ones((SGU_CHUNK, SGU_CHUNK), dtype=bool))
    out = None
    for l in range(depth):
        w_mix = _pack_mixer_weights(w_in, l)
        sguw = jnp.concatenate([jnp.where(causal, sgu_w[l, hh], 0.0) for hh in range(SGU_HEADS)],
                               axis=0).astype(BF16)
        sgub = jnp.repeat(sgu_b[l].T, SGU_HEAD_DIM, axis=1).astype(F32)
        pw = jax.scipy.linalg.block_diag(*[pool_w[l, gi] for gi in range(len(POOL_WINDOWS))]).astype(BF16)
        wg2 = jnp.pad(gla_wg2[l], ((0, LANES - GLA_RANK), (0, 0))).astype(BF16)
        gn = jnp.tile(gla_norm[l], GLA_HEADS)
        y = _mixer_call(x, _row(norm_mix[l]), w_mix, sguw, sgub, _row(sgu_norm[l]),
                        conv_w[l].astype(F32), _row(conv_b[l]), pw, _row(pool_scale[l]),
                        wg2, _row(gla_bg[l]), _row(gn))
        i = l // 2
        x2 = _merge_call(x.reshape(t, d), y.reshape(t, N_BRANCH * BRANCH_WIDTH), _row(norm_mix[l]),
                         w_in[l, :, GATE_OFF:].astype(BF16), _row(b_gate[l]),
                         branch_proj[l].reshape(N_BRANCH * BRANCH_WIDTH, d).astype(BF16),
                         w_out[l].astype(BF16))
        if l % 2 == 0:
            x2 = _ffn_call(x2, _row(norm_ffn[l]), ffn_w1[i].astype(BF16), ffn_w3[i].astype(BF16),
                           ffn_w2[i].astype(BF16))
            if l == depth - 1:
                out = _final_norm_call(x2, _row(final_norm))
        else:
            rw = jnp.pad(moe_router[i], ((0, 0), (0, LANES - N_EXPERTS)))
            rhi = rw.astype(BF16)
            h2, route = _router_call(x2, _row(norm_ffn[l]), rhi, (rw - rhi.astype(F32)).astype(BF16))
            n_tiles = (t * TOP_K) // GROUP_TILE + N_EXPERTS
            tile_expert, tile_valid, row_token, pos0, pos1 = _routing_plan(route, n_tiles)
            ys = _group_call(tile_expert, tile_valid, row_token, h2,
                             moe_w1[i].astype(BF16), moe_w3[i].astype(BF16), moe_w2[i].astype(BF16))
            x2 = _combine_call(pos0, pos1, ys, x2, route, _row(final_norm), l == depth - 1)
            if l == depth - 1:
                out = x2
        x = x2.reshape(b, s, d)
    return out.reshape(b, s, d)
```

```python
import functools

import numpy as np
import jax
import jax.numpy as jnp
from jax import lax
from jax.experimental import pallas as pl
from jax.experimental.pallas import tpu as pltpu

F32 = jnp.float32
BF16 = jnp.bfloat16

D_MODEL = 1024
N_BRANCH = 4
BRANCH_WIDTH = 256
SGU_HEADS = 4
SGU_HEAD_DIM = BRANCH_WIDTH // SGU_HEADS
SGU_CHUNK = 128
CONV_WIDTH = 3
POOL_WINDOWS = (2, 4, 8, 16)
POOL_GROUP_DIM = BRANCH_WIDTH // len(POOL_WINDOWS)
POOL_HISTORY = 16
POOL_PAD = 8
assert POOL_WINDOWS == (2, 4, 8, 16)
CONV_HISTORY = 8
GLA_HEADS = 4
GLA_DK = 32
GLA_DV = BRANCH_WIDTH // GLA_HEADS
GLA_HK = GLA_HEADS * GLA_DK
GLA_RANK = 16
GLA_TAU = 16.0
GLA_CHUNK = 64
D_FF = 2816
N_EXPERTS = 8
TOP_K = 2
EPS = 1e-6

A_COLS = 2 * BRANCH_WIDTH
B_COLS = 3 * BRANCH_WIDTH
C_COLS = BRANCH_WIDTH
QKVR_COLS = 2 * GLA_HK + 2 * BRANCH_WIDTH
LANES = 128
OFF_A = 0
OFF_B = OFF_A + A_COLS
OFF_C = OFF_B + B_COLS
OFF_D = OFF_C + C_COLS
OFF_GLR = OFF_D + QKVR_COLS
MIX_COLS = OFF_GLR + LANES
GATE_OFF = OFF_GLR + GLA_RANK

MIX_TS = 512
ROW_TILE = 512
GROUP_TILE = 512
FF_CHUNK = 256
VMEM_LIMIT = 56 * 1024 * 1024


def _dot(a, b):
    return jnp.dot(a, b, preferred_element_type=F32)


def _dot_nt(a, b):
    return lax.dot_general(a, b, (((1,), (1,)), ((), ())), preferred_element_type=F32)


def _dot_tn(a, b):
    return lax.dot_general(a, b, (((0,), (0,)), ((), ())), preferred_element_type=F32)


def _split(x):
    hi = x.astype(BF16)
    lo = (x - hi.astype(F32)).astype(BF16)
    return hi, lo


def _rms(x, g):
    ms = jnp.mean(x * x, axis=-1, keepdims=True)
    return x * lax.rsqrt(ms + EPS) * g


def _sigmoid(x):
    return 1.0 / (1.0 + jnp.exp(-x))


def _silu(x):
    return x * _sigmoid(x)


def _gelu_tanh(x):
    c = np.float32(np.sqrt(2.0 / np.pi))
    return x * (0.5 * (1.0 + jnp.tanh(c * (x + 0.044715 * (x * x * x)))))


def _group_mean_matrix(n, group):
    r = lax.broadcasted_iota(jnp.int32, (n, n), 0) // group
    c = lax.broadcasted_iota(jnp.int32, (n, n), 1) // group
    return jnp.where(r == c, 1.0 / group, 0.0).astype(BF16)


def _group_rms(x, bd, g):
    hi, lo = _split(x * x)
    ms = _dot(hi, bd) + _dot(lo, bd)
    return x * lax.rsqrt(ms + EPS) * g


def _mixer_kernel(x_ref, g_ref, w_ref, sguw_ref, sgub_ref, sgun_ref, cw_ref, cb_ref, pw_ref, ps_ref,
                  wg2_ref, bg_ref, gn_ref, *rest, n_cast):
    cast_in = rest[:n_cast]
    y_ref = rest[n_cast]
    cast_out = rest[n_cast + 1:2 * n_cast + 1]
    state_ref, ybuf_ref, zbuf_ref = rest[2 * n_cast + 1:]
    for src, dst in zip(cast_in, cast_out):
        dst[...] = src[...].astype(BF16)
    si = pl.program_id(1)
    ts = x_ref.shape[1]
    x = x_ref[0]
    h = _rms(x, g_ref[...]).astype(BF16)

    @pl.when(si == 0)
    def _():
        state_ref[...] = jnp.zeros_like(state_ref)
        ybuf_ref[0:CONV_HISTORY, :] = jnp.zeros((CONV_HISTORY, BRANCH_WIDTH), F32)
        zbuf_ref[:, 0:POOL_PAD + POOL_HISTORY, :] = jnp.zeros((3, POOL_PAD + POOL_HISTORY, BRANCH_WIDTH), F32)

    bd64 = _group_mean_matrix(BRANCH_WIDTH, SGU_HEAD_DIM)
    lane = lax.broadcasted_iota(jnp.int32, (1, BRANCH_WIDTH), 1)

    z = _gelu_tanh(_dot(h, w_ref[:, OFF_A:OFF_A + A_COLS]))
    u = z[:, :BRANCH_WIDTH]
    vn = _group_rms(z[:, BRANCH_WIDTH:], bd64, sgun_ref[...])
    lane_head = lane // SGU_HEAD_DIM
    vnb = vn.astype(BF16)
    for c in range(ts // SGU_CHUNK):
        rows = slice(c * SGU_CHUNK, (c + 1) * SGU_CHUNK)
        stacked = _dot(sguw_ref[...], vnb[rows])
        mixed = stacked[(SGU_HEADS - 1) * SGU_CHUNK:]
        for hh in range(SGU_HEADS - 2, -1, -1):
            mixed = jnp.where(lane_head == hh, stacked[hh * SGU_CHUNK:(hh + 1) * SGU_CHUNK], mixed)
        y_ref[0, rows, 0:BRANCH_WIDTH] = (u[rows] * (mixed + sgub_ref[...])).astype(BF16)

    zb = _dot(h, w_ref[:, OFF_B:OFF_B + B_COLS])
    gate_b = zb[:, BRANCH_WIDTH:2 * BRANCH_WIDTH]
    yv = zb[:, 2 * BRANCH_WIDTH:] * zb[:, :BRANCH_WIDTH]
    ybuf_ref[CONV_HISTORY:CONV_HISTORY + ts, :] = yv
    conv = cb_ref[...] + yv * cw_ref[CONV_WIDTH - 1:CONV_WIDTH, :]
    for i in range(CONV_WIDTH - 1):
        back = CONV_WIDTH - 1 - i
        conv = conv + ybuf_ref[CONV_HISTORY - back:CONV_HISTORY - back + ts, :] * cw_ref[i:i + 1, :]
    ybuf_ref[0:CONV_HISTORY, :] = ybuf_ref[ts:ts + CONV_HISTORY, :]
    y_ref[0, :, BRANCH_WIDTH:2 * BRANCH_WIDTH] = (gate_b * conv).astype(BF16)

    zc = _dot(h, w_ref[:, OFF_C:OFF_C + C_COLS])
    ext = POOL_HISTORY + ts
    p0 = POOL_PAD
    zbuf_ref[0, p0 + POOL_HISTORY:p0 + ext, :] = zc
    s2 = zbuf_ref[0, p0:p0 + ext, :] + zbuf_ref[0, p0 - 1:p0 - 1 + ext, :]
    zbuf_ref[1, p0:p0 + ext, :] = s2
    s4 = s2 + zbuf_ref[1, p0 - 2:p0 - 2 + ext, :]
    zbuf_ref[2, p0:p0 + ext, :] = s4
    s8 = s4 + zbuf_ref[2, p0 - 4:p0 - 4 + ext, :]
    sums = {2: s2[POOL_HISTORY:], 4: s4[POOL_HISTORY:], 8: s8[POOL_HISTORY:],
            16: s8[POOL_HISTORY:] + s8[POOL_HISTORY - 8:POOL_HISTORY - 8 + ts]}
    zbuf_ref[0, p0:p0 + POOL_HISTORY, :] = zbuf_ref[0, p0 + ts:p0 + ext, :]
    lane_group = lane // POOL_GROUP_DIM
    wsum = sums[POOL_WINDOWS[-1]]
    win = jnp.full((1, BRANCH_WIDTH), POOL_WINDOWS[-1], jnp.int32)
    for gi in range(len(POOL_WINDOWS) - 2, -1, -1):
        wsum = jnp.where(lane_group == gi, sums[POOL_WINDOWS[gi]], wsum)
        win = jnp.where(lane_group == gi, POOL_WINDOWS[gi], win)
    tpos = si * ts + lax.broadcasted_iota(jnp.int32, (ts, 1), 0)
    count = jnp.minimum(tpos + 1, win).astype(F32)
    pooled = wsum / count - zc
    y_ref[0, :, 2 * BRANCH_WIDTH:3 * BRANCH_WIDTH] = (
        _dot(pooled.astype(BF16), pw_ref[...]) * ps_ref[...]).astype(BF16)

    zd = _dot(h, w_ref[:, OFF_D:OFF_D + QKVR_COLS + LANES])
    q = zd[:, 0:GLA_HK]
    k = zd[:, GLA_HK:2 * GLA_HK]
    v = zd[:, 2 * GLA_HK:2 * GLA_HK + BRANCH_WIDTH]
    r = zd[:, 2 * GLA_HK + BRANCH_WIDTH:QKVR_COLS]
    glr = zd[:, QKVR_COLS:]
    a = _dot(glr.astype(BF16), wg2_ref[...]) + bg_ref[...]
    log_g = (jnp.minimum(a, 0.0) - jnp.log(1.0 + jnp.exp(-jnp.abs(a)))) * (1.0 / GLA_TAU)
    blk = 4 * GLA_CHUNK
    rr = lax.broadcasted_iota(jnp.int32, (blk, blk), 0)
    cc = lax.broadcasted_iota(jnp.int32, (blk, blk), 1)
    same = (rr // GLA_CHUNK) == (cc // GLA_CHUNK)
    tri = jnp.where(same & (cc <= rr), 1.0, 0.0).astype(BF16)
    ones = jnp.where(same, 1.0, 0.0).astype(BF16)
    cums, lasts = [], []
    for b in range(ts // blk):
        hi, lo = _split(log_g[b * blk:(b + 1) * blk])
        cums.append(_dot(tri, hi) + _dot(tri, lo))
        lasts.append(_dot(ones, hi) + _dot(ones, lo))
    cum = jnp.concatenate(cums, axis=0)
    last = jnp.concatenate(lasts, axis=0)
    qd_all = (q * (GLA_DK ** -0.5) * jnp.exp(cum)).astype(BF16)
    ki_all = (k * jnp.exp(-cum)).astype(BF16)
    ke_all = (k * jnp.exp(last - cum)).astype(BF16)
    v_all = v.astype(BF16)
    decay_all = jnp.exp(last)

    nrow = GLA_HEADS * GLA_CHUNK
    row_head = lax.broadcasted_iota(jnp.int32, (nrow, 1), 0) // GLA_CHUNK
    mask_k = jnp.where(row_head == lax.broadcasted_iota(jnp.int32, (nrow, GLA_HK), 1) // GLA_DK,
                       1.0, 0.0).astype(BF16)
    mask_v = jnp.where(row_head == lax.broadcasted_iota(jnp.int32, (nrow, BRANCH_WIDTH), 1) // GLA_DV,
                       1.0, 0.0).astype(BF16)
    causal = (lax.broadcasted_iota(jnp.int32, (GLA_CHUNK, nrow), 1) % GLA_CHUNK
              <= lax.broadcasted_iota(jnp.int32, (GLA_CHUNK, nrow), 0))
    mask_s = (lax.broadcasted_iota(jnp.int32, (BRANCH_WIDTH, GLA_HK), 0) // GLA_DV
              == lax.broadcasted_iota(jnp.int32, (BRANCH_WIDTH, GLA_HK), 1) // GLA_DK)

    state = state_ref[...]
    outs = []
    for n in range(ts // GLA_CHUNK):
        rows = slice(n * GLA_CHUNK, (n + 1) * GLA_CHUNK)
        qd = qd_all[rows]
        vc = v_all[rows]
        kbd = jnp.concatenate([ki_all[rows]] * GLA_HEADS, axis=0) * mask_k
        vbd = jnp.concatenate([vc] * GLA_HEADS, axis=0) * mask_v
        scores = jnp.where(causal, _dot_nt(qd, kbd), 0.0)
        outs.append(_dot(scores.astype(BF16), vbd) + _dot_nt(qd, state.astype(BF16)))
        kv = _dot_tn(vc, ke_all[rows])
        state = state * decay_all[n * GLA_CHUNK:n * GLA_CHUNK + 1, :] + jnp.where(mask_s, kv, 0.0)
    state_ref[...] = state
    on = _group_rms(jnp.concatenate(outs, axis=0), bd64, gn_ref[...])
    y_ref[0, :, 3 * BRANCH_WIDTH:] = (_silu(r) * on).astype(BF16)


def _mixer_call(x, g, w_mix, sguw, sgub, sgun, cw, cb, pw, ps, wg2, bg, gn, cast_slabs=()):
    b, s, d = x.shape
    ts = MIX_TS
    n_si = s // ts
    steps = b * n_si

    def const(arr):
        return pl.BlockSpec(arr.shape, lambda bi, si: (0,) * arr.ndim)

    def slab(arr):
        return pl.BlockSpec((arr.shape[0] // steps, arr.shape[1]), lambda bi, si: (bi * n_si + si, 0))

    consts = (g, w_mix, sguw, sgub, sgun, cw, cb, pw, ps, wg2, bg, gn)
    return pl.pallas_call(
        functools.partial(_mixer_kernel, n_cast=len(cast_slabs)),
        grid=(b, n_si),
        in_specs=([pl.BlockSpec((1, ts, d), lambda bi, si: (bi, si, 0))] + [const(c) for c in consts]
                  + [slab(a) for a in cast_slabs]),
        out_specs=([pl.BlockSpec((1, ts, N_BRANCH * BRANCH_WIDTH), lambda bi, si: (bi, si, 0))]
                   + [slab(a) for a in cast_slabs]),
        out_shape=([jax.ShapeDtypeStruct((b, s, N_BRANCH * BRANCH_WIDTH), BF16)]
                   + [jax.ShapeDtypeStruct(a.shape, BF16) for a in cast_slabs]),
        scratch_shapes=[
            pltpu.VMEM((BRANCH_WIDTH, GLA_HK), F32),
            pltpu.VMEM((CONV_HISTORY + ts, BRANCH_WIDTH), F32),
            pltpu.VMEM((3, POOL_PAD + POOL_HISTORY + ts, BRANCH_WIDTH), F32),
        ],
        compiler_params=pltpu.CompilerParams(
            dimension_semantics=("arbitrary", "arbitrary"), vmem_limit_bytes=VMEM_LIMIT),
        name="mixer",
    )(x, *consts, *cast_slabs)


def _merge_kernel(x_ref, y_ref, g_ref, wg_ref, bgate_ref, p_ref, wo_ref, o_ref):
    x = x_ref[...]
    h = _rms(x, g_ref[...]).astype(BF16)
    merged = None
    for i in range(N_BRANCH):
        cols = slice(i * D_MODEL, (i + 1) * D_MODEL)
        gate = _sigmoid(_dot(h, wg_ref[:, cols]) + bgate_ref[:, cols])
        rows = slice(i * BRANCH_WIDTH, (i + 1) * BRANCH_WIDTH)
        term = gate * _dot(y_ref[:, rows], p_ref[rows, :])
        merged = term if merged is None else merged + term
    o_ref[...] = x + _dot(merged.astype(BF16), wo_ref[...])


def _row_spec(tm, width):
    return pl.BlockSpec((tm, width), lambda i: (i, 0))


def _const_spec(arr):
    return pl.BlockSpec(arr.shape, lambda i: (0,) * arr.ndim)


def _merge_call(x2, y2, g, wg, bgate, p, wo):
    t, d = x2.shape
    tm = ROW_TILE
    consts = (g, wg, bgate, p, wo)
    return pl.pallas_call(
        _merge_kernel,
        grid=(t // tm,),
        in_specs=[_row_spec(tm, d), _row_spec(tm, y2.shape[1])] + [_const_spec(c) for c in consts],
        out_specs=_row_spec(tm, d),
        out_shape=jax.ShapeDtypeStruct((t, d), F32),
        compiler_params=pltpu.CompilerParams(
            dimension_semantics=("arbitrary",), vmem_limit_bytes=VMEM_LIMIT),
        name="merge",
    )(x2, y2, *consts)


FF_CHUNKS = D_FF // FF_CHUNK


def _swiglu_rows(hb, w1_ref, w3_ref, w2_ref, lead):
    acc = None
    for f in range(FF_CHUNKS):
        cols = slice(f * FF_CHUNK, (f + 1) * FF_CHUNK)
        up = _dot(hb, w1_ref[lead + (slice(None), cols)])
        gt = _dot(hb, w3_ref[lead + (slice(None), cols)])
        part = _dot((_silu(up) * gt).astype(BF16), w2_ref[lead + (cols, slice(None))])
        acc = part if acc is None else acc + part
    return acc


def _ffn_kernel(x_ref, g_ref, w1_ref, w3_ref, w2_ref, o_ref):
    x = x_ref[...]
    hb = _rms(x, g_ref[...]).astype(BF16)
    o_ref[...] = x + _swiglu_rows(hb, w1_ref, w3_ref, w2_ref, ())


def _ffn_call(x2, g, w1, w3, w2):
    t, d = x2.shape
    tm = ROW_TILE
    consts = (g, w1, w3, w2)
    return pl.pallas_call(
        _ffn_kernel,
        grid=(t // tm,),
        in_specs=[_row_spec(tm, d)] + [_const_spec(c) for c in consts],
        out_specs=_row_spec(tm, d),
        out_shape=jax.ShapeDtypeStruct((t, d), F32),
        compiler_params=pltpu.CompilerParams(
            dimension_semantics=("arbitrary",), vmem_limit_bytes=VMEM_LIMIT),
        name="ffn",
    )(x2, *consts)


ROUTE_IDX0 = N_EXPERTS
ROUTE_W0 = N_EXPERTS + TOP_K
D_TILES = D_MODEL // LANES


def _to_token_tiles(ref, value, lead=()):
    rows = value.shape[0]
    for c in range(D_TILES):
        ref[lead + (pl.ds(c, rows, stride=D_TILES), slice(None))] = value[:, c * LANES:(c + 1) * LANES]


def _from_token_tiles(ref, rows, lead=()):
    return jnp.concatenate(
        [ref[lead + (pl.ds(c, rows, stride=D_TILES), slice(None))] for c in range(D_TILES)], axis=1)


def _route_rows(x, g_ref, rhi_ref, rlo_ref, h_ref, route_ref):
    h = _rms(x, g_ref[...])
    _to_token_tiles(h_ref, h)
    hi, lo = _split(h)
    logits = _dot(hi, rhi_ref[...]) + _dot(hi, rlo_ref[...]) + _dot(lo, rhi_ref[...])
    lane = lax.broadcasted_iota(jnp.int32, logits.shape, 1)
    neg = jnp.float32(-jnp.inf)
    logits = jnp.where(lane < N_EXPERTS, logits, neg)
    m1 = jnp.max(logits, axis=-1, keepdims=True)
    i1 = jnp.min(jnp.where(logits == m1, lane, LANES), axis=-1, keepdims=True)
    rest = jnp.where(lane == i1, neg, logits)
    m2 = jnp.max(rest, axis=-1, keepdims=True)
    i2 = jnp.min(jnp.where(rest == m2, lane, LANES), axis=-1, keepdims=True)
    e2 = jnp.exp(m2 - m1)
    w1 = 1.0 / (1.0 + e2)
    w2 = e2 / (1.0 + e2)
    out = jnp.where(lane == ROUTE_IDX0, i1.astype(F32), 0.0)
    out = jnp.where(lane == ROUTE_IDX0 + 1, i2.astype(F32), out)
    out = jnp.where(lane == ROUTE_W0, w1, out)
    out = jnp.where(lane == ROUTE_W0 + 1, w2, out)
    route_ref[...] = out


def _router_kernel(x_ref, g_ref, rhi_ref, rlo_ref, h_ref, route_ref):
    _route_rows(x_ref[...], g_ref, rhi_ref, rlo_ref, h_ref, route_ref)


def _router_call(x2, g, rhi, rlo):
    t, d = x2.shape
    tm = ROW_TILE
    consts = (g, rhi, rlo)
    return pl.pallas_call(
        _router_kernel,
        grid=(t // tm,),
        in_specs=[_row_spec(tm, d)] + [_const_spec(c) for c in consts],
        out_specs=[_row_spec(tm * D_TILES, LANES), _row_spec(tm, LANES)],
        out_shape=[jax.ShapeDtypeStruct((t * D_TILES, LANES), F32), jax.ShapeDtypeStruct((t, LANES), F32)],
        compiler_params=pltpu.CompilerParams(
            dimension_semantics=("arbitrary",), vmem_limit_bytes=VMEM_LIMIT),
        name="router",
    )(x2, *consts)


GATHER_UNROLL = 16


def _token_tile(ref, r):
    return ref.at[pl.ds(pl.multiple_of(r * D_TILES, D_TILES), D_TILES), :]


def _start_row_gather(idx_ref, src_hbm, dst, sem):
    def body(r, c):
        pltpu.make_async_copy(_token_tile(src_hbm, idx_ref[0, 0, r]), _token_tile(dst, r), sem).start()
        return c

    lax.fori_loop(0, dst.shape[0] // D_TILES, body, 0, unroll=GATHER_UNROLL)


def _wait_row_gather(src_hbm, dst, sem):
    pltpu.make_async_copy(src_hbm.at[pl.ds(0, dst.shape[0]), :], dst, sem).wait()


def _group_kernel(texp_ref, tvalid_ref, rows_ref, rows_next_ref, h_hbm, w1_ref, w3_ref, w2_ref, ys_ref,
                  xbuf, sem):
    i = pl.program_id(0)
    slot = i % 2

    @pl.when(i == 0)
    def _():
        _start_row_gather(rows_ref, h_hbm, xbuf.at[0], sem.at[0])

    @pl.when(i + 1 < pl.num_programs(0))
    def _():
        _start_row_gather(rows_next_ref, h_hbm, xbuf.at[1 - slot], sem.at[1 - slot])

    _wait_row_gather(h_hbm, xbuf.at[slot], sem.at[slot])

    @pl.when(tvalid_ref[i] == 1)
    def _():
        xb = _from_token_tiles(xbuf, GROUP_TILE, (slot,)).astype(BF16)
        _to_token_tiles(ys_ref, _swiglu_rows(xb, w1_ref, w3_ref, w2_ref, (0,)))

    @pl.when(tvalid_ref[i] == 0)
    def _():
        ys_ref[...] = jnp.zeros_like(ys_ref)


def _group_call(tile_expert, tile_valid, row_token, h3, w1, w3, w2):
    n_tiles = tile_expert.shape[0]
    tm = GROUP_TILE
    rows = row_token.reshape(n_tiles, 1, tm)

    def wspec(arr):
        return pl.BlockSpec((1,) + arr.shape[1:], lambda i, te, tv: (te[i], 0, 0))

    grid_spec = pltpu.PrefetchScalarGridSpec(
        num_scalar_prefetch=2,
        grid=(n_tiles,),
        in_specs=[
            pl.BlockSpec((1, 1, tm), lambda i, te, tv: (i, 0, 0), memory_space=pltpu.SMEM),
            pl.BlockSpec((1, 1, tm), lambda i, te, tv: (jnp.minimum(i + 1, n_tiles - 1), 0, 0),
                         memory_space=pltpu.SMEM),
            pl.BlockSpec(memory_space=pl.ANY),
            wspec(w1), wspec(w3), wspec(w2),
        ],
        out_specs=pl.BlockSpec((tm * D_TILES, LANES), lambda i, te, tv: (i, 0)),
        scratch_shapes=[pltpu.VMEM((2, tm * D_TILES, LANES), F32), pltpu.SemaphoreType.DMA((2,))],
    )
    return pl.pallas_call(
        _group_kernel,
        grid_spec=grid_spec,
        out_shape=jax.ShapeDtypeStruct((n_tiles * tm * D_TILES, LANES), F32),
        compiler_params=pltpu.CompilerParams(
            dimension_semantics=("arbitrary",), vmem_limit_bytes=VMEM_LIMIT),
        name="moe_group",
    )(tile_expert, tile_valid, rows, rows, h3, w1, w3, w2)


def _combine_kernel(p0_ref, p1_ref, p0_next_ref, p1_next_ref, ys_hbm, x_ref, route_ref, g_ref, o_ref,
                    buf, sem, *, apply_final_norm):
    i = pl.program_id(0)
    slot = i % 2

    def start(refs, s):
        for j in range(TOP_K):
            _start_row_gather(refs[j], ys_hbm, buf.at[s, j], sem.at[s, j])

    @pl.when(i == 0)
    def _():
        start((p0_ref, p1_ref), 0)

    @pl.when(i + 1 < pl.num_programs(0))
    def _():
        start((p0_next_ref, p1_next_ref), 1 - slot)

    for j in range(TOP_K):
        _wait_row_gather(ys_hbm, buf.at[slot, j], sem.at[slot, j])
    route = route_ref[...]
    y = x_ref[...]
    for j in range(TOP_K):
        y = y + route[:, ROUTE_W0 + j:ROUTE_W0 + j + 1] * _from_token_tiles(buf, ROW_TILE, (slot, j))
    o_ref[...] = _rms(y, g_ref[...]) if apply_final_norm else y


def _combine_call(pos0, pos1, ys, x2, route, g, apply_final_norm):
    t, d = x2.shape
    tm = ROW_TILE
    n = t // tm

    def ispec(shift):
        return pl.BlockSpec((1, 1, tm), lambda i: (jnp.minimum(i + shift, n - 1), 0, 0),
                            memory_space=pltpu.SMEM)

    p0 = pos0.reshape(n, 1, tm)
    p1 = pos1.reshape(n, 1, tm)
    return pl.pallas_call(
        functools.partial(_combine_kernel, apply_final_norm=apply_final_norm),
        grid=(n,),
        in_specs=[ispec(0), ispec(0), ispec(1), ispec(1), pl.BlockSpec(memory_space=pl.ANY),
                  _row_spec(tm, d), _row_spec(tm, LANES), _const_spec(g)],
        out_specs=_row_spec(tm, d),
        out_shape=jax.ShapeDtypeStruct((t, d), F32),
        scratch_shapes=[pltpu.VMEM((2, TOP_K, tm * D_TILES, LANES), F32),
                        pltpu.SemaphoreType.DMA((2, TOP_K))],
        compiler_params=pltpu.CompilerParams(
            dimension_semantics=("arbitrary",), vmem_limit_bytes=VMEM_LIMIT),
        name="moe_combine",
    )(p0, p1, p0, p1, ys, x2, route, g)


def _final_norm_kernel(x_ref, g_ref, o_ref):
    o_ref[...] = _rms(x_ref[...], g_ref[...])


def _final_norm_call(x2, g):
    t, d = x2.shape
    tm = ROW_TILE
    return pl.pallas_call(
        _final_norm_kernel,
        grid=(t // tm,),
        in_specs=[_row_spec(tm, d), _const_spec(g)],
        out_specs=_row_spec(tm, d),
        out_shape=jax.ShapeDtypeStruct((t, d), F32),
        name="final_norm",
    )(x2, g)


def _routing_plan(route, n_tiles):
    t = route.shape[0]
    tm = GROUP_TILE
    experts = route[:, ROUTE_IDX0:ROUTE_IDX0 + TOP_K].astype(jnp.int32).reshape(t * TOP_K)
    onehot = (experts[:, None] == jnp.arange(N_EXPERTS, dtype=jnp.int32)[None, :]).astype(jnp.int32)
    rank = jnp.sum((jnp.cumsum(onehot, axis=0) - onehot) * onehot, axis=1)
    counts = jnp.sum(onehot, axis=0)
    tiles_per = (counts + tm - 1) // tm
    tile_end = jnp.cumsum(tiles_per)
    tile_start = tile_end - tiles_per
    pos = tile_start[experts] * tm + rank
    tile_ids = jnp.arange(n_tiles, dtype=jnp.int32)
    tile_expert = jnp.minimum(
        jnp.sum((tile_ids[:, None] >= tile_end[None, :]).astype(jnp.int32), axis=1), N_EXPERTS - 1)
    tile_valid = (tile_ids < tile_end[-1]).astype(jnp.int32)
    order = jnp.argsort(experts, stable=True).astype(jnp.int32)
    row_expert = jnp.repeat(tile_expert, tm)
    row_rank = jnp.arange(n_tiles * tm, dtype=jnp.int32) - tile_start[row_expert] * tm
    row_used = (row_rank < counts[row_expert]) & (jnp.repeat(tile_valid, tm) == 1)
    src = jnp.clip((jnp.cumsum(counts) - counts)[row_expert] + row_rank, 0, t * TOP_K - 1)
    row_token = jnp.where(row_used, order[src] // TOP_K, 0)
    last_expert = tile_expert[jnp.maximum(tile_end[-1] - 1, 0)]
    tile_expert = jnp.where(tile_valid == 1, tile_expert, last_expert)
    pos2 = pos.reshape(t, TOP_K)
    return tile_expert, tile_valid, row_token, pos2[:, 0], pos2[:, 1]


def _pack_mixer_weights(w_in):
    glr = jnp.pad(w_in[:, OFF_GLR:GATE_OFF], ((0, 0), (0, LANES - GLA_RANK)))
    return jnp.concatenate([w_in[:, :OFF_GLR], glr], axis=1).astype(BF16)


def _row(v):
    return v.reshape(1, -1).astype(F32)


def kernel(x, norm_mix, w_in, b_gate, sgu_w, sgu_b, sgu_norm, conv_w, conv_b, pool_w, pool_scale, gla_wg2, gla_bg, gla_norm, branch_proj, w_out, norm_ffn, ffn_w1, ffn_w3, ffn_w2, moe_router, moe_w1, moe_w3, moe_w2, final_norm):
    b, s, d = x.shape
    t = b * s
    depth = norm_mix.shape[0]
    causal = jnp.tril(jnp.ones((SGU_CHUNK, SGU_CHUNK), dtype=bool))
    out = None
    for l in range(depth):
        w_mix = _pack_mixer_weights(w_in[l])
        sguw = jnp.concatenate([jnp.where(causal, sgu_w[l, hh], 0.0) for hh in range(SGU_HEADS)],
                               axis=0).astype(BF16)
        sgub = jnp.repeat(sgu_b[l].T, SGU_HEAD_DIM, axis=1).astype(F32)
        pw = jax.scipy.linalg.block_diag(*[pool_w[l, gi] for gi in range(len(POOL_WINDOWS))]).astype(BF16)
        wg2 = jnp.pad(gla_wg2[l], ((0, LANES - GLA_RANK), (0, 0))).astype(BF16)
        gn = jnp.tile(gla_norm[l], GLA_HEADS)
        slabs = []
        if l % 2 == 0 and l + 1 < depth:
            slabs = [w.reshape(-1, w.shape[-1]) for w in (moe_w1[(l + 1) // 2], moe_w3[(l + 1) // 2])]
        elif l % 2 == 1:
            slabs = [moe_w2[l // 2].reshape(-1, d)]
        y, *cast = _mixer_call(x, _row(norm_mix[l]), w_mix, sguw, sgub, _row(sgu_norm[l]),
                               conv_w[l].astype(F32), _row(conv_b[l]), pw, _row(pool_scale[l]),
                               wg2, _row(gla_bg[l]), _row(gn), slabs)
        if l % 2 == 0 and l + 1 < depth:
            moe_up = [c.reshape(N_EXPERTS, d, D_FF) for c in cast]
        elif l % 2 == 1:
            moe_down = cast[0].reshape(N_EXPERTS, D_FF, d)
        i = l // 2
        x2 = _merge_call(x.reshape(t, d), y.reshape(t, N_BRANCH * BRANCH_WIDTH), _row(norm_mix[l]),
                         w_in[l][:, GATE_OFF:].astype(BF16), _row(b_gate[l]),
                         branch_proj[l].reshape(N_BRANCH * BRANCH_WIDTH, d).astype(BF16),
                         w_out[l].astype(BF16))
        if l % 2 == 0:
            x2 = _ffn_call(x2, _row(norm_ffn[l]), ffn_w1[i].astype(BF16), ffn_w3[i].astype(BF16),
                           ffn_w2[i].astype(BF16))
            if l == depth - 1:
                out = _final_norm_call(x2, _row(final_norm))
        else:
            rw = jnp.pad(moe_router[i], ((0, 0), (0, LANES - N_EXPERTS)))
            rhi = rw.astype(BF16)
            h2, route = _router_call(x2, _row(norm_ffn[l]), rhi, (rw - rhi.astype(F32)).astype(BF16))
            n_tiles = (t * TOP_K) // GROUP_TILE + N_EXPERTS
            tile_expert, tile_valid, row_token, pos0, pos1 = _routing_plan(route, n_tiles)
            ys = _group_call(tile_expert, tile_valid, row_token, h2, moe_up[0], moe_up[1], moe_down)
            x2 = _combine_call(pos0, pos1, ys, x2, route, _row(final_norm), l == depth - 1)
            if l == depth - 1:
                out = x2
        x = x2.reshape(b, s, d)
    return out.reshape(b, s, d)
```

```python
import functools

import numpy as np
import jax
import jax.numpy as jnp
from jax import lax
from jax.experimental import pallas as pl
from jax.experimental.pallas import tpu as pltpu

F32 = jnp.float32
BF16 = jnp.bfloat16

D_MODEL = 1024
N_BRANCH = 4
BRANCH_WIDTH = 256
SGU_HEADS = 4
SGU_HEAD_DIM = BRANCH_WIDTH // SGU_HEADS
SGU_CHUNK = 128
CONV_WIDTH = 3
POOL_WINDOWS = (2, 4, 8, 16)
POOL_GROUP_DIM = BRANCH_WIDTH // len(POOL_WINDOWS)
POOL_HISTORY = 16
POOL_PAD = 8
assert POOL_WINDOWS == (2, 4, 8, 16)
CONV_HISTORY = 8
GLA_HEADS = 4
GLA_DK = 32
GLA_DV = BRANCH_WIDTH // GLA_HEADS
GLA_HK = GLA_HEADS * GLA_DK
GLA_RANK = 16
GLA_TAU = 16.0
GLA_CHUNK = 64
D_FF = 2816
N_EXPERTS = 8
TOP_K = 2
EPS = 1e-6

A_COLS = 2 * BRANCH_WIDTH
B_COLS = 3 * BRANCH_WIDTH
C_COLS = BRANCH_WIDTH
QKVR_COLS = 2 * GLA_HK + 2 * BRANCH_WIDTH
LANES = 128
OFF_A = 0
OFF_B = OFF_A + A_COLS
OFF_C = OFF_B + B_COLS
OFF_D = OFF_C + C_COLS
OFF_GLR = OFF_D + QKVR_COLS
MIX_COLS = OFF_GLR + LANES
GATE_OFF = OFF_GLR + GLA_RANK

MIX_TS = 512
ROW_TILE = 512
GROUP_TILE = 512
FF_CHUNK = 256
VMEM_LIMIT = 56 * 1024 * 1024


def _dot(a, b):
    return jnp.dot(a, b, preferred_element_type=F32)


def _dot_nt(a, b):
    return lax.dot_general(a, b, (((1,), (1,)), ((), ())), preferred_element_type=F32)


def _dot_tn(a, b):
    return lax.dot_general(a, b, (((0,), (0,)), ((), ())), preferred_element_type=F32)


def _split(x):
    hi = x.astype(BF16)
    lo = (x - hi.astype(F32)).astype(BF16)
    return hi, lo


def _rms(x, g):
    ms = jnp.mean(x * x, axis=-1, keepdims=True)
    return x * lax.rsqrt(ms + EPS) * g


def _sigmoid(x):
    return 1.0 / (1.0 + jnp.exp(-x))


def _silu(x):
    return x * _sigmoid(x)


def _gelu_tanh(x):
    c = np.float32(np.sqrt(2.0 / np.pi))
    return x * (0.5 * (1.0 + jnp.tanh(c * (x + 0.044715 * (x * x * x)))))


def _group_mean_matrix(n, group):
    r = lax.broadcasted_iota(jnp.int32, (n, n), 0) // group
    c = lax.broadcasted_iota(jnp.int32, (n, n), 1) // group
    return jnp.where(r == c, 1.0 / group, 0.0).astype(BF16)


def _group_rms(x, bd, g):
    hi, lo = _split(x * x)
    ms = _dot(hi, bd) + _dot(lo, bd)
    return x * lax.rsqrt(ms + EPS) * g


def _mixer_kernel(x_ref, g_ref, w_ref, sguw_ref, sgub_ref, sgun_ref, cw_ref, cb_ref, pw_ref, ps_ref,
                  wg2_ref, bg_ref, gn_ref, *rest, n_cast):
    cast_in = rest[:n_cast]
    y_ref = rest[n_cast]
    cast_out = rest[n_cast + 1:2 * n_cast + 1]
    state_ref, ybuf_ref, zbuf_ref = rest[2 * n_cast + 1:]
    for src, dst in zip(cast_in, cast_out):
        dst[...] = src[...].astype(BF16)
    si = pl.program_id(1)
    ts = x_ref.shape[1]
    x = x_ref[0]
    h = _rms(x, g_ref[...]).astype(BF16)

    @pl.when(si == 0)
    def _():
        state_ref[...] = jnp.zeros_like(state_ref)
        ybuf_ref[0:CONV_HISTORY, :] = jnp.zeros((CONV_HISTORY, BRANCH_WIDTH), F32)
        zbuf_ref[:, 0:POOL_PAD + POOL_HISTORY, :] = jnp.zeros((3, POOL_PAD + POOL_HISTORY, BRANCH_WIDTH), F32)

    bd64 = _group_mean_matrix(BRANCH_WIDTH, SGU_HEAD_DIM)
    lane = lax.broadcasted_iota(jnp.int32, (1, BRANCH_WIDTH), 1)

    z = _gelu_tanh(_dot(h, w_ref[:, OFF_A:OFF_A + A_COLS]))
    u = z[:, :BRANCH_WIDTH]
    vn = _group_rms(z[:, BRANCH_WIDTH:], bd64, sgun_ref[...])
    lane_head = lane // SGU_HEAD_DIM
    vnb = vn.astype(BF16)
    for c in range(ts // SGU_CHUNK):
        rows = slice(c * SGU_CHUNK, (c + 1) * SGU_CHUNK)
        stacked = _dot(sguw_ref[...], vnb[rows])
        mixed = stacked[(SGU_HEADS - 1) * SGU_CHUNK:]
        for hh in range(SGU_HEADS - 2, -1, -1):
            mixed = jnp.where(lane_head == hh, stacked[hh * SGU_CHUNK:(hh + 1) * SGU_CHUNK], mixed)
        y_ref[0, rows, 0:BRANCH_WIDTH] = (u[rows] * (mixed + sgub_ref[...])).astype(BF16)

    zb = _dot(h, w_ref[:, OFF_B:OFF_B + B_COLS])
    gate_b = zb[:, BRANCH_WIDTH:2 * BRANCH_WIDTH]
    yv = zb[:, 2 * BRANCH_WIDTH:] * zb[:, :BRANCH_WIDTH]
    ybuf_ref[CONV_HISTORY:CONV_HISTORY + ts, :] = yv
    conv = cb_ref[...] + yv * cw_ref[CONV_WIDTH - 1:CONV_WIDTH, :]
    for i in range(CONV_WIDTH - 1):
        back = CONV_WIDTH - 1 - i
        conv = conv + ybuf_ref[CONV_HISTORY - back:CONV_HISTORY - back + ts, :] * cw_ref[i:i + 1, :]
    ybuf_ref[0:CONV_HISTORY, :] = ybuf_ref[ts:ts + CONV_HISTORY, :]
    y_ref[0, :, BRANCH_WIDTH:2 * BRANCH_WIDTH] = (gate_b * conv).astype(BF16)

    zc = _dot(h, w_ref[:, OFF_C:OFF_C + C_COLS])
    ext = POOL_HISTORY + ts
    p0 = POOL_PAD
    zbuf_ref[0, p0 + POOL_HISTORY:p0 + ext, :] = zc
    s2 = zbuf_ref[0, p0:p0 + ext, :] + zbuf_ref[0, p0 - 1:p0 - 1 + ext, :]
    zbuf_ref[1, p0:p0 + ext, :] = s2
    s4 = s2 + zbuf_ref[1, p0 - 2:p0 - 2 + ext, :]
    zbuf_ref[2, p0:p0 + ext, :] = s4
    s8 = s4 + zbuf_ref[2, p0 - 4:p0 - 4 + ext, :]
    sums = {2: s2[POOL_HISTORY:], 4: s4[POOL_HISTORY:], 8: s8[POOL_HISTORY:],
            16: s8[POOL_HISTORY:] + s8[POOL_HISTORY - 8:POOL_HISTORY - 8 + ts]}
    zbuf_ref[0, p0:p0 + POOL_HISTORY, :] = zbuf_ref[0, p0 + ts:p0 + ext, :]
    lane_group = lane // POOL_GROUP_DIM
    wsum = sums[POOL_WINDOWS[-1]]
    win = jnp.full((1, BRANCH_WIDTH), POOL_WINDOWS[-1], jnp.int32)
    for gi in range(len(POOL_WINDOWS) - 2, -1, -1):
        wsum = jnp.where(lane_group == gi, sums[POOL_WINDOWS[gi]], wsum)
        win = jnp.where(lane_group == gi, POOL_WINDOWS[gi], win)
    tpos = si * ts + lax.broadcasted_iota(jnp.int32, (ts, 1), 0)
    count = jnp.minimum(tpos + 1, win).astype(F32)
    pooled = wsum / count - zc
    y_ref[0, :, 2 * BRANCH_WIDTH:3 * BRANCH_WIDTH] = (
        _dot(pooled.astype(BF16), pw_ref[...]) * ps_ref[...]).astype(BF16)

    zd = _dot(h, w_ref[:, OFF_D:OFF_D + QKVR_COLS + LANES])
    q = zd[:, 0:GLA_HK]
    k = zd[:, GLA_HK:2 * GLA_HK]
    v = zd[:, 2 * GLA_HK:2 * GLA_HK + BRANCH_WIDTH]
    r = zd[:, 2 * GLA_HK + BRANCH_WIDTH:QKVR_COLS]
    glr = zd[:, QKVR_COLS:]
    a = _dot(glr.astype(BF16), wg2_ref[...]) + bg_ref[...]
    log_g = (jnp.minimum(a, 0.0) - jnp.log(1.0 + jnp.exp(-jnp.abs(a)))) * (1.0 / GLA_TAU)
    blk = 4 * GLA_CHUNK
    rr = lax.broadcasted_iota(jnp.int32, (blk, blk), 0)
    cc = lax.broadcasted_iota(jnp.int32, (blk, blk), 1)
    same = (rr // GLA_CHUNK) == (cc // GLA_CHUNK)
    tri = jnp.where(same & (cc <= rr), 1.0, 0.0).astype(BF16)
    ones = jnp.where(same, 1.0, 0.0).astype(BF16)
    cums, lasts = [], []
    for b in range(ts // blk):
        hi, lo = _split(log_g[b * blk:(b + 1) * blk])
        cums.append(_dot(tri, hi) + _dot(tri, lo))
        lasts.append(_dot(ones, hi) + _dot(ones, lo))
    cum = jnp.concatenate(cums, axis=0)
    last = jnp.concatenate(lasts, axis=0)
    qd_all = (q * (GLA_DK ** -0.5) * jnp.exp(cum)).astype(BF16)
    ki_all = (k * jnp.exp(-cum)).astype(BF16)
    ke_all = (k * jnp.exp(last - cum)).astype(BF16)
    v_all = v.astype(BF16)
    decay_all = jnp.exp(last)

    nrow = GLA_HEADS * GLA_CHUNK
    row_head = lax.broadcasted_iota(jnp.int32, (nrow, 1), 0) // GLA_CHUNK
    mask_k = jnp.where(row_head == lax.broadcasted_iota(jnp.int32, (nrow, GLA_HK), 1) // GLA_DK,
                       1.0, 0.0).astype(BF16)
    mask_v = jnp.where(row_head == lax.broadcasted_iota(jnp.int32, (nrow, BRANCH_WIDTH), 1) // GLA_DV,
                       1.0, 0.0).astype(BF16)
    causal = (lax.broadcasted_iota(jnp.int32, (GLA_CHUNK, nrow), 1) % GLA_CHUNK
              <= lax.broadcasted_iota(jnp.int32, (GLA_CHUNK, nrow), 0))
    mask_s = (lax.broadcasted_iota(jnp.int32, (BRANCH_WIDTH, GLA_HK), 0) // GLA_DV
              == lax.broadcasted_iota(jnp.int32, (BRANCH_WIDTH, GLA_HK), 1) // GLA_DK)

    state = state_ref[...]
    outs = []
    for n in range(ts // GLA_CHUNK):
        rows = slice(n * GLA_CHUNK, (n + 1) * GLA_CHUNK)
        qd = qd_all[rows]
        vc = v_all[rows]
        kbd = jnp.concatenate([ki_all[rows]] * GLA_HEADS, axis=0) * mask_k
        vbd = jnp.concatenate([vc] * GLA_HEADS, axis=0) * mask_v
        scores = jnp.where(causal, _dot_nt(qd, kbd), 0.0)
        outs.append(_dot(scores.astype(BF16), vbd) + _dot_nt(qd, state.astype(BF16)))
        kv = _dot_tn(vc, ke_all[rows])
        state = state * decay_all[n * GLA_CHUNK:n * GLA_CHUNK + 1, :] + jnp.where(mask_s, kv, 0.0)
    state_ref[...] = state
    on = _group_rms(jnp.concatenate(outs, axis=0), bd64, gn_ref[...])
    y_ref[0, :, 3 * BRANCH_WIDTH:] = (_silu(r) * on).astype(BF16)


def _mixer_call(x, g, w_mix, sguw, sgub, sgun, cw, cb, pw, ps, wg2, bg, gn, cast_slabs=()):
    b, s, d = x.shape
    ts = MIX_TS
    n_si = s // ts
    steps = b * n_si

    def const(arr):
        return pl.BlockSpec(arr.shape, lambda bi, si: (0,) * arr.ndim)

    def slab(arr):
        return pl.BlockSpec((arr.shape[0] // steps, arr.shape[1]), lambda bi, si: (bi * n_si + si, 0))

    consts = (g, w_mix, sguw, sgub, sgun, cw, cb, pw, ps, wg2, bg, gn)
    return pl.pallas_call(
        functools.partial(_mixer_kernel, n_cast=len(cast_slabs)),
        grid=(b, n_si),
        in_specs=([pl.BlockSpec((1, ts, d), lambda bi, si: (bi, si, 0))] + [const(c) for c in consts]
                  + [slab(a) for a in cast_slabs]),
        out_specs=([pl.BlockSpec((1, ts, N_BRANCH * BRANCH_WIDTH), lambda bi, si: (bi, si, 0))]
                   + [slab(a) for a in cast_slabs]),
        out_shape=([jax.ShapeDtypeStruct((b, s, N_BRANCH * BRANCH_WIDTH), BF16)]
                   + [jax.ShapeDtypeStruct(a.shape, BF16) for a in cast_slabs]),
        scratch_shapes=[
            pltpu.VMEM((BRANCH_WIDTH, GLA_HK), F32),
            pltpu.VMEM((CONV_HISTORY + ts, BRANCH_WIDTH), F32),
            pltpu.VMEM((3, POOL_PAD + POOL_HISTORY + ts, BRANCH_WIDTH), F32),
        ],
        compiler_params=pltpu.CompilerParams(
            dimension_semantics=("arbitrary", "arbitrary"), vmem_limit_bytes=VMEM_LIMIT),
        name="mixer",
    )(x, *consts, *cast_slabs)


def _merge_kernel(x_ref, y_ref, g_ref, wg_ref, bgate_ref, p_ref, wo_ref, o_ref):
    x = x_ref[...]
    h = _rms(x, g_ref[...]).astype(BF16)
    merged = None
    for i in range(N_BRANCH):
        cols = slice(i * D_MODEL, (i + 1) * D_MODEL)
        gate = _sigmoid(_dot(h, wg_ref[:, cols]) + bgate_ref[:, cols])
        rows = slice(i * BRANCH_WIDTH, (i + 1) * BRANCH_WIDTH)
        term = gate * _dot(y_ref[:, rows], p_ref[rows, :])
        merged = term if merged is None else merged + term
    o_ref[...] = x + _dot(merged.astype(BF16), wo_ref[...])


def _row_spec(tm, width):
    return pl.BlockSpec((tm, width), lambda i: (i, 0))


def _const_spec(arr):
    return pl.BlockSpec(arr.shape, lambda i: (0,) * arr.ndim)


def _merge_call(x2, y2, g, wg, bgate, p, wo):
    t, d = x2.shape
    tm = ROW_TILE
    consts = (g, wg, bgate, p, wo)
    return pl.pallas_call(
        _merge_kernel,
        grid=(t // tm,),
        in_specs=[_row_spec(tm, d), _row_spec(tm, y2.shape[1])] + [_const_spec(c) for c in consts],
        out_specs=_row_spec(tm, d),
        out_shape=jax.ShapeDtypeStruct((t, d), F32),
        compiler_params=pltpu.CompilerParams(
            dimension_semantics=("arbitrary",), vmem_limit_bytes=VMEM_LIMIT),
        name="merge",
    )(x2, y2, *consts)


FF_CHUNKS = D_FF // FF_CHUNK


def _swiglu_rows(hb, w1_ref, w3_ref, w2_ref, lead):
    acc = None
    for f in range(FF_CHUNKS):
        cols = slice(f * FF_CHUNK, (f + 1) * FF_CHUNK)
        up = _dot(hb, w1_ref[lead + (slice(None), cols)])
        gt = _dot(hb, w3_ref[lead + (slice(None), cols)])
        part = _dot((_silu(up) * gt).astype(BF16), w2_ref[lead + (cols, slice(None))])
        acc = part if acc is None else acc + part
    return acc


def _ffn_kernel(x_ref, g_ref, w1_ref, w3_ref, w2_ref, o_ref):
    x = x_ref[...]
    hb = _rms(x, g_ref[...]).astype(BF16)
    o_ref[...] = x + _swiglu_rows(hb, w1_ref, w3_ref, w2_ref, ())


def _ffn_call(x2, g, w1, w3, w2):
    t, d = x2.shape
    tm = ROW_TILE
    consts = (g, w1, w3, w2)
    return pl.pallas_call(
        _ffn_kernel,
        grid=(t // tm,),
        in_specs=[_row_spec(tm, d)] + [_const_spec(c) for c in consts],
        out_specs=_row_spec(tm, d),
        out_shape=jax.ShapeDtypeStruct((t, d), F32),
        compiler_params=pltpu.CompilerParams(
            dimension_semantics=("arbitrary",), vmem_limit_bytes=VMEM_LIMIT),
        name="ffn",
    )(x2, *consts)


ROUTE_IDX0 = N_EXPERTS
ROUTE_W0 = N_EXPERTS + TOP_K
ROUTE_RANK0 = N_EXPERTS + 2 * TOP_K
SUBLANES = 8
D_TILES = D_MODEL // LANES


def _to_token_tiles(ref, value, lead=()):
    rows = value.shape[0]
    for c in range(D_TILES):
        ref[lead + (pl.ds(c, rows, stride=D_TILES), slice(None))] = value[:, c * LANES:(c + 1) * LANES]


def _from_token_tiles(ref, rows, lead=()):
    return jnp.concatenate(
        [ref[lead + (pl.ds(c, rows, stride=D_TILES), slice(None))] for c in range(D_TILES)], axis=1)


def _router_kernel(x_ref, g_ref, rhi_ref, rlo_ref, h_ref, route_ref, route_t_ref, counts_ref, run_ref):
    step = pl.program_id(0)

    @pl.when(step == 0)
    def _():
        run_ref[...] = jnp.zeros_like(run_ref)

    x = x_ref[...]
    tm = x.shape[0]
    h = _rms(x, g_ref[...])
    _to_token_tiles(h_ref, h)
    hi, lo = _split(h)
    logits = _dot(hi, rhi_ref[...]) + _dot(hi, rlo_ref[...]) + _dot(lo, rhi_ref[...])
    lane = lax.broadcasted_iota(jnp.int32, logits.shape, 1)
    neg = jnp.float32(-jnp.inf)
    logits = jnp.where(lane < N_EXPERTS, logits, neg)
    m1 = jnp.max(logits, axis=-1, keepdims=True)
    i1 = jnp.min(jnp.where(logits == m1, lane, LANES), axis=-1, keepdims=True)
    rest = jnp.where(lane == i1, neg, logits)
    m2 = jnp.max(rest, axis=-1, keepdims=True)
    i2 = jnp.min(jnp.where(rest == m2, lane, LANES), axis=-1, keepdims=True)
    e2 = jnp.exp(m2 - m1)
    w1 = 1.0 / (1.0 + e2)
    w2 = e2 / (1.0 + e2)
    out = jnp.where(lane == ROUTE_IDX0, i1.astype(F32), 0.0)
    out = jnp.where(lane == ROUTE_IDX0 + 1, i2.astype(F32), out)
    out = jnp.where(lane == ROUTE_W0, w1, out)
    out = jnp.where(lane == ROUTE_W0 + 1, w2, out)
    before = jnp.where(lax.broadcasted_iota(jnp.int32, (tm, tm), 1) < lax.broadcasted_iota(jnp.int32, (tm, tm), 0),
                       1.0, 0.0).astype(BF16)
    for j, idx in enumerate((i1, i2)):
        onehot = jnp.where(lane == idx, 1.0, 0.0)
        seen = _dot(before, onehot.astype(BF16)) + run_ref[j:j + 1, :]
        rank = jnp.sum(onehot * seen, axis=-1, keepdims=True)
        out = jnp.where(lane == ROUTE_RANK0 + j, rank, out)
        run_ref[j:j + 1, :] = run_ref[j:j + 1, :] + jnp.sum(onehot, axis=0, keepdims=True)
    route_ref[...] = out
    route_t_ref[...] = out.T
    counts_ref[...] = run_ref[...]


def _router_call(x2, g, rhi, rlo):
    t, d = x2.shape
    tm = ROW_TILE
    consts = (g, rhi, rlo)
    return pl.pallas_call(
        _router_kernel,
        grid=(t // tm,),
        in_specs=[_row_spec(tm, d)] + [_const_spec(c) for c in consts],
        out_specs=[_row_spec(tm * D_TILES, LANES), _row_spec(tm, LANES),
                   pl.BlockSpec((LANES, tm), lambda i: (0, i)), pl.BlockSpec((SUBLANES, LANES), lambda i: (0, 0))],
        out_shape=[jax.ShapeDtypeStruct((t * D_TILES, LANES), F32), jax.ShapeDtypeStruct((t, LANES), F32),
                   jax.ShapeDtypeStruct((LANES, t), F32), jax.ShapeDtypeStruct((SUBLANES, LANES), F32)],
        scratch_shapes=[pltpu.VMEM((SUBLANES, LANES), F32)],
        compiler_params=pltpu.CompilerParams(
            dimension_semantics=("arbitrary",), vmem_limit_bytes=VMEM_LIMIT),
        name="router",
    )(x2, *consts)


GATHER_UNROLL = 16


def _token_tile(ref, r):
    return ref.at[pl.ds(pl.multiple_of(r * D_TILES, D_TILES), D_TILES), :]


def _start_row_gather(idx_ref, src_hbm, dst, sem):
    def body(r, c):
        pltpu.make_async_copy(_token_tile(src_hbm, idx_ref[0, 0, r]), _token_tile(dst, r), sem).start()
        return c

    lax.fori_loop(0, dst.shape[0] // D_TILES, body, 0, unroll=GATHER_UNROLL)


def _wait_row_gather(src_hbm, dst, sem):
    pltpu.make_async_copy(src_hbm.at[pl.ds(0, dst.shape[0]), :], dst, sem).wait()


def _group_kernel(texp_ref, tvalid_ref, rows_ref, rows_next_ref, h_hbm, w1_ref, w3_ref, w2_ref, ys_ref,
                  xbuf, sem):
    i = pl.program_id(0)
    slot = i % 2

    @pl.when(i == 0)
    def _():
        _start_row_gather(rows_ref, h_hbm, xbuf.at[0], sem.at[0])

    @pl.when(i + 1 < pl.num_programs(0))
    def _():
        _start_row_gather(rows_next_ref, h_hbm, xbuf.at[1 - slot], sem.at[1 - slot])

    _wait_row_gather(h_hbm, xbuf.at[slot], sem.at[slot])

    @pl.when(tvalid_ref[i] == 1)
    def _():
        xb = _from_token_tiles(xbuf, GROUP_TILE, (slot,)).astype(BF16)
        _to_token_tiles(ys_ref, _swiglu_rows(xb, w1_ref, w3_ref, w2_ref, (0,)))

    @pl.when(tvalid_ref[i] == 0)
    def _():
        ys_ref[...] = jnp.zeros_like(ys_ref)


def _group_call(tile_expert, tile_valid, row_token, h3, w1, w3, w2):
    n_tiles = tile_expert.shape[0]
    tm = GROUP_TILE
    rows = row_token.reshape(n_tiles, 1, tm)

    def wspec(arr):
        return pl.BlockSpec((1,) + arr.shape[1:], lambda i, te, tv: (te[i], 0, 0))

    grid_spec = pltpu.PrefetchScalarGridSpec(
        num_scalar_prefetch=2,
        grid=(n_tiles,),
        in_specs=[
            pl.BlockSpec((1, 1, tm), lambda i, te, tv: (i, 0, 0), memory_space=pltpu.SMEM),
            pl.BlockSpec((1, 1, tm), lambda i, te, tv: (jnp.minimum(i + 1, n_tiles - 1), 0, 0),
                         memory_space=pltpu.SMEM),
            pl.BlockSpec(memory_space=pl.ANY),
            wspec(w1), wspec(w3), wspec(w2),
        ],
        out_specs=pl.BlockSpec((tm * D_TILES, LANES), lambda i, te, tv: (i, 0)),
        scratch_shapes=[pltpu.VMEM((2, tm * D_TILES, LANES), F32), pltpu.SemaphoreType.DMA((2,))],
    )
    return pl.pallas_call(
        _group_kernel,
        grid_spec=grid_spec,
        out_shape=jax.ShapeDtypeStruct((n_tiles * tm * D_TILES, LANES), F32),
        compiler_params=pltpu.CompilerParams(
            dimension_semantics=("arbitrary",), vmem_limit_bytes=VMEM_LIMIT),
        name="moe_group",
    )(tile_expert, tile_valid, rows, rows, h3, w1, w3, w2)


def _combine_kernel(p0_ref, p1_ref, p0_next_ref, p1_next_ref, ys_hbm, x_ref, route_ref, g_ref, o_ref,
                    buf, sem, *, apply_final_norm):
    i = pl.program_id(0)
    slot = i % 2

    def start(refs, s):
        for j in range(TOP_K):
            _start_row_gather(refs[j], ys_hbm, buf.at[s, j], sem.at[s, j])

    @pl.when(i == 0)
    def _():
        start((p0_ref, p1_ref), 0)

    @pl.when(i + 1 < pl.num_programs(0))
    def _():
        start((p0_next_ref, p1_next_ref), 1 - slot)

    for j in range(TOP_K):
        _wait_row_gather(ys_hbm, buf.at[slot, j], sem.at[slot, j])
    route = route_ref[...]
    y = x_ref[...]
    for j in range(TOP_K):
        y = y + route[:, ROUTE_W0 + j:ROUTE_W0 + j + 1] * _from_token_tiles(buf, ROW_TILE, (slot, j))
    o_ref[...] = _rms(y, g_ref[...]) if apply_final_norm else y


def _combine_call(pos0, pos1, ys, x2, route, g, apply_final_norm):
    t, d = x2.shape
    tm = ROW_TILE
    n = t // tm

    def ispec(shift):
        return pl.BlockSpec((1, 1, tm), lambda i: (jnp.minimum(i + shift, n - 1), 0, 0),
                            memory_space=pltpu.SMEM)

    p0 = pos0.reshape(n, 1, tm)
    p1 = pos1.reshape(n, 1, tm)
    return pl.pallas_call(
        functools.partial(_combine_kernel, apply_final_norm=apply_final_norm),
        grid=(n,),
        in_specs=[ispec(0), ispec(0), ispec(1), ispec(1), pl.BlockSpec(memory_space=pl.ANY),
                  _row_spec(tm, d), _row_spec(tm, LANES), _const_spec(g)],
        out_specs=_row_spec(tm, d),
        out_shape=jax.ShapeDtypeStruct((t, d), F32),
        scratch_shapes=[pltpu.VMEM((2, TOP_K, tm * D_TILES, LANES), F32),
                        pltpu.SemaphoreType.DMA((2, TOP_K))],
        compiler_params=pltpu.CompilerParams(
            dimension_semantics=("arbitrary",), vmem_limit_bytes=VMEM_LIMIT),
        name="moe_combine",
    )(p0, p1, p0, p1, ys, x2, route, g)


def _final_norm_kernel(x_ref, g_ref, o_ref):
    o_ref[...] = _rms(x_ref[...], g_ref[...])


def _final_norm_call(x2, g):
    t, d = x2.shape
    tm = ROW_TILE
    return pl.pallas_call(
        _final_norm_kernel,
        grid=(t // tm,),
        in_specs=[_row_spec(tm, d), _const_spec(g)],
        out_specs=_row_spec(tm, d),
        out_shape=jax.ShapeDtypeStruct((t, d), F32),
        name="final_norm",
    )(x2, g)


def _routing_plan(route_t, choice_counts, n_tiles):
    t = route_t.shape[1]
    tm = GROUP_TILE
    choice = [route_t[ROUTE_IDX0 + j].astype(jnp.int32) for j in range(TOP_K)]
    rank = [route_t[ROUTE_RANK0 + j].astype(jnp.int32) for j in range(TOP_K)]
    per_choice = choice_counts[:TOP_K, :N_EXPERTS].astype(jnp.int32)
    counts = jnp.sum(per_choice, axis=0)
    tiles_per = (counts + tm - 1) // tm
    tile_end = jnp.cumsum(tiles_per)
    tile_start = tile_end - tiles_per
    earlier = jnp.cumsum(per_choice, axis=0) - per_choice
    pos = [tile_start[choice[j]] * tm + earlier[j][choice[j]] + rank[j] for j in range(TOP_K)]
    experts = jnp.concatenate(choice)
    tile_ids = jnp.arange(n_tiles, dtype=jnp.int32)
    tile_expert = jnp.minimum(
        jnp.sum((tile_ids[:, None] >= tile_end[None, :]).astype(jnp.int32), axis=1), N_EXPERTS - 1)
    tile_valid = (tile_ids < tile_end[-1]).astype(jnp.int32)
    order = jnp.argsort(experts, stable=True).astype(jnp.int32)
    row_expert = jnp.repeat(tile_expert, tm)
    row_rank = jnp.arange(n_tiles * tm, dtype=jnp.int32) - tile_start[row_expert] * tm
    row_used = (row_rank < counts[row_expert]) & (jnp.repeat(tile_valid, tm) == 1)
    src = jnp.clip((jnp.cumsum(counts) - counts)[row_expert] + row_rank, 0, t * TOP_K - 1)
    row_token = jnp.where(row_used, order[src] % t, 0)
    last_expert = tile_expert[jnp.maximum(tile_end[-1] - 1, 0)]
    tile_expert = jnp.where(tile_valid == 1, tile_expert, last_expert)
    return tile_expert, tile_valid, row_token, pos[0], pos[1]


def _pack_mixer_weights(w_in):
    glr = jnp.pad(w_in[:, OFF_GLR:GATE_OFF], ((0, 0), (0, LANES - GLA_RANK)))
    return jnp.concatenate([w_in[:, :OFF_GLR], glr], axis=1).astype(BF16)


def _row(v):
    return v.reshape(1, -1).astype(F32)


def kernel(x, norm_mix, w_in, b_gate, sgu_w, sgu_b, sgu_norm, conv_w, conv_b, pool_w, pool_scale, gla_wg2, gla_bg, gla_norm, branch_proj, w_out, norm_ffn, ffn_w1, ffn_w3, ffn_w2, moe_router, moe_w1, moe_w3, moe_w2, final_norm):
    b, s, d = x.shape
    t = b * s
    depth = norm_mix.shape[0]
    causal = jnp.tril(jnp.ones((SGU_CHUNK, SGU_CHUNK), dtype=bool))
    out = None
    for l in range(depth):
        w_mix = _pack_mixer_weights(w_in[l])
        sguw = jnp.concatenate([jnp.where(causal, sgu_w[l, hh], 0.0) for hh in range(SGU_HEADS)],
                               axis=0).astype(BF16)
        sgub = jnp.repeat(sgu_b[l].T, SGU_HEAD_DIM, axis=1).astype(F32)
        pw = jax.scipy.linalg.block_diag(*[pool_w[l, gi] for gi in range(len(POOL_WINDOWS))]).astype(BF16)
        wg2 = jnp.pad(gla_wg2[l], ((0, LANES - GLA_RANK), (0, 0))).astype(BF16)
        gn = jnp.tile(gla_norm[l], GLA_HEADS)
        slabs = []
        if l % 2 == 0 and l + 1 < depth:
            slabs = [w.reshape(-1, w.shape[-1]) for w in (moe_w1[(l + 1) // 2], moe_w3[(l + 1) // 2])]
        elif l % 2 == 1:
            slabs = [moe_w2[l // 2].reshape(-1, d)]
        y, *cast = _mixer_call(x, _row(norm_mix[l]), w_mix, sguw, sgub, _row(sgu_norm[l]),
                               conv_w[l].astype(F32), _row(conv_b[l]), pw, _row(pool_scale[l]),
                               wg2, _row(gla_bg[l]), _row(gn), slabs)
        if l % 2 == 0 and l + 1 < depth:
            moe_up = [c.reshape(N_EXPERTS, d, D_FF) for c in cast]
        elif l % 2 == 1:
            moe_down = cast[0].reshape(N_EXPERTS, D_FF, d)
        i = l // 2
        x2 = _merge_call(x.reshape(t, d), y.reshape(t, N_BRANCH * BRANCH_WIDTH), _row(norm_mix[l]),
                         w_in[l][:, GATE_OFF:].astype(BF16), _row(b_gate[l]),
                         branch_proj[l].reshape(N_BRANCH * BRANCH_WIDTH, d).astype(BF16),
                         w_out[l].astype(BF16))
        if l % 2 == 0:
            x2 = _ffn_call(x2, _row(norm_ffn[l]), ffn_w1[i].astype(BF16), ffn_w3[i].astype(BF16),
                           ffn_w2[i].astype(BF16))
            if l == depth - 1:
                out = _final_norm_call(x2, _row(final_norm))
        else:
            rw = jnp.pad(moe_router[i], ((0, 0), (0, LANES - N_EXPERTS)))
            rhi = rw.astype(BF16)
            h2, route, route_t, choice_counts = _router_call(
                x2, _row(norm_ffn[l]), rhi, (rw - rhi.astype(F32)).astype(BF16))
            n_tiles = (t * TOP_K) // GROUP_TILE + N_EXPERTS
            tile_expert, tile_valid, row_token, pos0, pos1 = _routing_plan(route_t, choice_counts, n_tiles)
            ys = _group_call(tile_expert, tile_valid, row_token, h2, moe_up[0], moe_up[1], moe_down)
            x2 = _combine_call(pos0, pos1, ys, x2, route, _row(final_norm), l == depth - 1)
            if l == depth - 1:
                out = x2
        x = x2.reshape(b, s, d)
    return out.reshape(b, s, d)
```

```python
import functools

import numpy as np
import jax
import jax.numpy as jnp
from jax import lax
from jax.experimental import pallas as pl
from jax.experimental.pallas import tpu as pltpu

F32 = jnp.float32
BF16 = jnp.bfloat16

D_MODEL = 1024
N_BRANCH = 4
BRANCH_WIDTH = 256
SGU_HEADS = 4
SGU_HEAD_DIM = BRANCH_WIDTH // SGU_HEADS
SGU_CHUNK = 128
CONV_WIDTH = 3
POOL_WINDOWS = (2, 4, 8, 16)
POOL_GROUP_DIM = BRANCH_WIDTH // len(POOL_WINDOWS)
POOL_HISTORY = 16
POOL_PAD = 8
assert POOL_WINDOWS == (2, 4, 8, 16)
CONV_HISTORY = 8
GLA_HEADS = 4
GLA_DK = 32
GLA_DV = BRANCH_WIDTH // GLA_HEADS
GLA_HK = GLA_HEADS * GLA_DK
GLA_RANK = 16
GLA_TAU = 16.0
GLA_CHUNK = 64
D_FF = 2816
N_EXPERTS = 8
TOP_K = 2
EPS = 1e-6

A_COLS = 2 * BRANCH_WIDTH
B_COLS = 3 * BRANCH_WIDTH
C_COLS = BRANCH_WIDTH
QKVR_COLS = 2 * GLA_HK + 2 * BRANCH_WIDTH
LANES = 128
OFF_A = 0
OFF_B = OFF_A + A_COLS
OFF_C = OFF_B + B_COLS
OFF_D = OFF_C + C_COLS
OFF_GLR = OFF_D + QKVR_COLS
MIX_COLS = OFF_GLR + LANES
GATE_OFF = OFF_GLR + GLA_RANK

MIX_TS = 512
ROW_TILE = 512
GROUP_TILE = 512
FF_CHUNK = 256
VMEM_LIMIT = 56 * 1024 * 1024


def _dot(a, b):
    return jnp.dot(a, b, preferred_element_type=F32)


def _dot_nt(a, b):
    return lax.dot_general(a, b, (((1,), (1,)), ((), ())), preferred_element_type=F32)


def _dot_tn(a, b):
    return lax.dot_general(a, b, (((0,), (0,)), ((), ())), preferred_element_type=F32)


def _split(x):
    hi = x.astype(BF16)
    lo = (x - hi.astype(F32)).astype(BF16)
    return hi, lo


def _rms(x, g):
    ms = jnp.mean(x * x, axis=-1, keepdims=True)
    return x * lax.rsqrt(ms + EPS) * g


def _sigmoid(x):
    return 1.0 / (1.0 + jnp.exp(-x))


def _silu(x):
    return x * _sigmoid(x)


def _gelu_tanh(x):
    c = np.float32(np.sqrt(2.0 / np.pi))
    return x * (0.5 * (1.0 + jnp.tanh(c * (x + 0.044715 * (x * x * x)))))


def _group_mean_matrix(n, group):
    r = lax.broadcasted_iota(jnp.int32, (n, n), 0) // group
    c = lax.broadcasted_iota(jnp.int32, (n, n), 1) // group
    return jnp.where(r == c, 1.0 / group, 0.0).astype(BF16)


def _group_rms(x, bd, g):
    hi, lo = _split(x * x)
    ms = _dot(hi, bd) + _dot(lo, bd)
    return x * lax.rsqrt(ms + EPS) * g


def _mixer_kernel(x_ref, g_ref, w_ref, sguw_ref, sgub_ref, sgun_ref, cw_ref, cb_ref, pw_ref, ps_ref,
                  wg2_ref, bg_ref, gn_ref, *rest, n_cast):
    cast_in = rest[:n_cast]
    y_ref = rest[n_cast]
    cast_out = rest[n_cast + 1:2 * n_cast + 1]
    state_ref, ybuf_ref, zbuf_ref = rest[2 * n_cast + 1:]
    for src, dst in zip(cast_in, cast_out):
        dst[...] = src[...].astype(BF16)
    si = pl.program_id(1)
    ts = x_ref.shape[1]
    x = x_ref[0]
    h = _rms(x, g_ref[...]).astype(BF16)

    @pl.when(si == 0)
    def _():
        state_ref[...] = jnp.zeros_like(state_ref)
        ybuf_ref[0:CONV_HISTORY, :] = jnp.zeros((CONV_HISTORY, BRANCH_WIDTH), F32)
        zbuf_ref[:, 0:POOL_PAD + POOL_HISTORY, :] = jnp.zeros((3, POOL_PAD + POOL_HISTORY, BRANCH_WIDTH), F32)

    bd64 = _group_mean_matrix(BRANCH_WIDTH, SGU_HEAD_DIM)
    lane = lax.broadcasted_iota(jnp.int32, (1, BRANCH_WIDTH), 1)

    z = _gelu_tanh(_dot(h, w_ref[:, OFF_A:OFF_A + A_COLS]))
    u = z[:, :BRANCH_WIDTH]
    vn = _group_rms(z[:, BRANCH_WIDTH:], bd64, sgun_ref[...])
    lane_head = lane // SGU_HEAD_DIM
    vnb = vn.astype(BF16)
    for c in range(ts // SGU_CHUNK):
        rows = slice(c * SGU_CHUNK, (c + 1) * SGU_CHUNK)
        stacked = _dot(sguw_ref[...], vnb[rows])
        mixed = stacked[(SGU_HEADS - 1) * SGU_CHUNK:]
        for hh in range(SGU_HEADS - 2, -1, -1):
            mixed = jnp.where(lane_head == hh, stacked[hh * SGU_CHUNK:(hh + 1) * SGU_CHUNK], mixed)
        y_ref[0, rows, 0:BRANCH_WIDTH] = (u[rows] * (mixed + sgub_ref[...])).astype(BF16)

    zb = _dot(h, w_ref[:, OFF_B:OFF_B + B_COLS])
    gate_b = zb[:, BRANCH_WIDTH:2 * BRANCH_WIDTH]
    yv = zb[:, 2 * BRANCH_WIDTH:] * zb[:, :BRANCH_WIDTH]
    ybuf_ref[CONV_HISTORY:CONV_HISTORY + ts, :] = yv
    conv = cb_ref[...] + yv * cw_ref[CONV_WIDTH - 1:CONV_WIDTH, :]
    for i in range(CONV_WIDTH - 1):
        back = CONV_WIDTH - 1 - i
        conv = conv + ybuf_ref[CONV_HISTORY - back:CONV_HISTORY - back + ts, :] * cw_ref[i:i + 1, :]
    ybuf_ref[0:CONV_HISTORY, :] = ybuf_ref[ts:ts + CONV_HISTORY, :]
    y_ref[0, :, BRANCH_WIDTH:2 * BRANCH_WIDTH] = (gate_b * conv).astype(BF16)

    zc = _dot(h, w_ref[:, OFF_C:OFF_C + C_COLS])
    ext = POOL_HISTORY + ts
    p0 = POOL_PAD
    zbuf_ref[0, p0 + POOL_HISTORY:p0 + ext, :] = zc
    s2 = zbuf_ref[0, p0:p0 + ext, :] + zbuf_ref[0, p0 - 1:p0 - 1 + ext, :]
    zbuf_ref[1, p0:p0 + ext, :] = s2
    s4 = s2 + zbuf_ref[1, p0 - 2:p0 - 2 + ext, :]
    zbuf_ref[2, p0:p0 + ext, :] = s4
    s8 = s4 + zbuf_ref[2, p0 - 4:p0 - 4 + ext, :]
    sums = {2: s2[POOL_HISTORY:], 4: s4[POOL_HISTORY:], 8: s8[POOL_HISTORY:],
            16: s8[POOL_HISTORY:] + s8[POOL_HISTORY - 8:POOL_HISTORY - 8 + ts]}
    zbuf_ref[0, p0:p0 + POOL_HISTORY, :] = zbuf_ref[0, p0 + ts:p0 + ext, :]
    lane_group = lane // POOL_GROUP_DIM
    wsum = sums[POOL_WINDOWS[-1]]
    win = jnp.full((1, BRANCH_WIDTH), POOL_WINDOWS[-1], jnp.int32)
    for gi in range(len(POOL_WINDOWS) - 2, -1, -1):
        wsum = jnp.where(lane_group == gi, sums[POOL_WINDOWS[gi]], wsum)
        win = jnp.where(lane_group == gi, POOL_WINDOWS[gi], win)
    tpos = si * ts + lax.broadcasted_iota(jnp.int32, (ts, 1), 0)
    count = jnp.minimum(tpos + 1, win).astype(F32)
    pooled = wsum / count - zc
    y_ref[0, :, 2 * BRANCH_WIDTH:3 * BRANCH_WIDTH] = (
        _dot(pooled.astype(BF16), pw_ref[...]) * ps_ref[...]).astype(BF16)

    zd = _dot(h, w_ref[:, OFF_D:OFF_D + QKVR_COLS + LANES])
    q = zd[:, 0:GLA_HK]
    k = zd[:, GLA_HK:2 * GLA_HK]
    v = zd[:, 2 * GLA_HK:2 * GLA_HK + BRANCH_WIDTH]
    r = zd[:, 2 * GLA_HK + BRANCH_WIDTH:QKVR_COLS]
    glr = zd[:, QKVR_COLS:]
    a = _dot(glr.astype(BF16), wg2_ref[...]) + bg_ref[...]
    log_g = (jnp.minimum(a, 0.0) - jnp.log(1.0 + jnp.exp(-jnp.abs(a)))) * (1.0 / GLA_TAU)
    blk = 4 * GLA_CHUNK
    rr = lax.broadcasted_iota(jnp.int32, (blk, blk), 0)
    cc = lax.broadcasted_iota(jnp.int32, (blk, blk), 1)
    same = (rr // GLA_CHUNK) == (cc // GLA_CHUNK)
    tri = jnp.where(same & (cc <= rr), 1.0, 0.0).astype(BF16)
    ones = jnp.where(same, 1.0, 0.0).astype(BF16)
    cums, lasts = [], []
    for b in range(ts // blk):
        hi, lo = _split(log_g[b * blk:(b + 1) * blk])
        cums.append(_dot(tri, hi) + _dot(tri, lo))
        lasts.append(_dot(ones, hi) + _dot(ones, lo))
    cum = jnp.concatenate(cums, axis=0)
    last = jnp.concatenate(lasts, axis=0)
    qd_all = (q * (GLA_DK ** -0.5) * jnp.exp(cum)).astype(BF16)
    ki_all = (k * jnp.exp(-cum)).astype(BF16)
    ke_all = (k * jnp.exp(last - cum)).astype(BF16)
    v_all = v.astype(BF16)
    decay_all = jnp.exp(last)

    nrow = GLA_HEADS * GLA_CHUNK
    row_head = lax.broadcasted_iota(jnp.int32, (nrow, 1), 0) // GLA_CHUNK
    mask_k = jnp.where(row_head == lax.broadcasted_iota(jnp.int32, (nrow, GLA_HK), 1) // GLA_DK,
                       1.0, 0.0).astype(BF16)
    mask_v = jnp.where(row_head == lax.broadcasted_iota(jnp.int32, (nrow, BRANCH_WIDTH), 1) // GLA_DV,
                       1.0, 0.0).astype(BF16)
    causal = (lax.broadcasted_iota(jnp.int32, (GLA_CHUNK, nrow), 1) % GLA_CHUNK
              <= lax.broadcasted_iota(jnp.int32, (GLA_CHUNK, nrow), 0))
    mask_s = (lax.broadcasted_iota(jnp.int32, (BRANCH_WIDTH, GLA_HK), 0) // GLA_DV
              == lax.broadcasted_iota(jnp.int32, (BRANCH_WIDTH, GLA_HK), 1) // GLA_DK)

    state = state_ref[...]
    outs = []
    for n in range(ts // GLA_CHUNK):
        rows = slice(n * GLA_CHUNK, (n + 1) * GLA_CHUNK)
        qd = qd_all[rows]
        vc = v_all[rows]
        kbd = jnp.concatenate([ki_all[rows]] * GLA_HEADS, axis=0) * mask_k
        vbd = jnp.concatenate([vc] * GLA_HEADS, axis=0) * mask_v
        scores = jnp.where(causal, _dot_nt(qd, kbd), 0.0)
        outs.append(_dot(scores.astype(BF16), vbd) + _dot_nt(qd, state.astype(BF16)))
        kv = _dot_tn(vc, ke_all[rows])
        state = state * decay_all[n * GLA_CHUNK:n * GLA_CHUNK + 1, :] + jnp.where(mask_s, kv, 0.0)
    state_ref[...] = state
    on = _group_rms(jnp.concatenate(outs, axis=0), bd64, gn_ref[...])
    y_ref[0, :, 3 * BRANCH_WIDTH:] = (_silu(r) * on).astype(BF16)


def _mixer_call(x, g, w_mix, sguw, sgub, sgun, cw, cb, pw, ps, wg2, bg, gn, cast_slabs=()):
    b, s, d = x.shape
    ts = MIX_TS
    n_si = s // ts
    steps = b * n_si

    def const(arr):
        return pl.BlockSpec(arr.shape, lambda bi, si: (0,) * arr.ndim)

    def slab(arr):
        return pl.BlockSpec((arr.shape[0] // steps, arr.shape[1]), lambda bi, si: (bi * n_si + si, 0))

    consts = (g, w_mix, sguw, sgub, sgun, cw, cb, pw, ps, wg2, bg, gn)
    return pl.pallas_call(
        functools.partial(_mixer_kernel, n_cast=len(cast_slabs)),
        grid=(b, n_si),
        in_specs=([pl.BlockSpec((1, ts, d), lambda bi, si: (bi, si, 0))] + [const(c) for c in consts]
                  + [slab(a) for a in cast_slabs]),
        out_specs=([pl.BlockSpec((1, ts, N_BRANCH * BRANCH_WIDTH), lambda bi, si: (bi, si, 0))]
                   + [slab(a) for a in cast_slabs]),
        out_shape=([jax.ShapeDtypeStruct((b, s, N_BRANCH * BRANCH_WIDTH), BF16)]
                   + [jax.ShapeDtypeStruct(a.shape, BF16) for a in cast_slabs]),
        scratch_shapes=[
            pltpu.VMEM((BRANCH_WIDTH, GLA_HK), F32),
            pltpu.VMEM((CONV_HISTORY + ts, BRANCH_WIDTH), F32),
            pltpu.VMEM((3, POOL_PAD + POOL_HISTORY + ts, BRANCH_WIDTH), F32),
        ],
        compiler_params=pltpu.CompilerParams(
            dimension_semantics=("arbitrary", "arbitrary"), vmem_limit_bytes=VMEM_LIMIT),
        name="mixer",
    )(x, *consts, *cast_slabs)


def _merge_kernel(x_ref, y_ref, g_ref, wg_ref, bgate_ref, p_ref, wo_ref, o_ref):
    x = x_ref[...]
    h = _rms(x, g_ref[...]).astype(BF16)
    merged = None
    for i in range(N_BRANCH):
        cols = slice(i * D_MODEL, (i + 1) * D_MODEL)
        gate = _sigmoid(_dot(h, wg_ref[:, cols]) + bgate_ref[:, cols])
        rows = slice(i * BRANCH_WIDTH, (i + 1) * BRANCH_WIDTH)
        term = gate * _dot(y_ref[:, rows], p_ref[rows, :])
        merged = term if merged is None else merged + term
    o_ref[...] = x + _dot(merged.astype(BF16), wo_ref[...])


def _row_spec(tm, width):
    return pl.BlockSpec((tm, width), lambda i: (i, 0))


def _const_spec(arr):
    return pl.BlockSpec(arr.shape, lambda i: (0,) * arr.ndim)


def _merge_call(x2, y2, g, wg, bgate, p, wo):
    t, d = x2.shape
    tm = ROW_TILE
    consts = (g, wg, bgate, p, wo)
    return pl.pallas_call(
        _merge_kernel,
        grid=(t // tm,),
        in_specs=[_row_spec(tm, d), _row_spec(tm, y2.shape[1])] + [_const_spec(c) for c in consts],
        out_specs=_row_spec(tm, d),
        out_shape=jax.ShapeDtypeStruct((t, d), F32),
        compiler_params=pltpu.CompilerParams(
            dimension_semantics=("arbitrary",), vmem_limit_bytes=VMEM_LIMIT),
        name="merge",
    )(x2, y2, *consts)


FF_CHUNKS = D_FF // FF_CHUNK


def _swiglu_rows(hb, w1_ref, w3_ref, w2_ref, lead):
    acc = None
    for f in range(FF_CHUNKS):
        cols = slice(f * FF_CHUNK, (f + 1) * FF_CHUNK)
        up = _dot(hb, w1_ref[lead + (slice(None), cols)])
        gt = _dot(hb, w3_ref[lead + (slice(None), cols)])
        part = _dot((_silu(up) * gt).astype(BF16), w2_ref[lead + (cols, slice(None))])
        acc = part if acc is None else acc + part
    return acc


def _ffn_kernel(x_ref, g_ref, w1_ref, w3_ref, w2_ref, o_ref):
    x = x_ref[...]
    hb = _rms(x, g_ref[...]).astype(BF16)
    o_ref[...] = x + _swiglu_rows(hb, w1_ref, w3_ref, w2_ref, ())


def _ffn_call(x2, g, w1, w3, w2):
    t, d = x2.shape
    tm = ROW_TILE
    consts = (g, w1, w3, w2)
    return pl.pallas_call(
        _ffn_kernel,
        grid=(t // tm,),
        in_specs=[_row_spec(tm, d)] + [_const_spec(c) for c in consts],
        out_specs=_row_spec(tm, d),
        out_shape=jax.ShapeDtypeStruct((t, d), F32),
        compiler_params=pltpu.CompilerParams(
            dimension_semantics=("arbitrary",), vmem_limit_bytes=VMEM_LIMIT),
        name="ffn",
    )(x2, *consts)


ROUTE_IDX0 = N_EXPERTS
ROUTE_W0 = N_EXPERTS + TOP_K
ROUTE_RANK0 = N_EXPERTS + 2 * TOP_K
CHOICE_LANES = LANES // TOP_K
SUBLANES = 8
D_TILES = D_MODEL // LANES


def _to_token_tiles(ref, value, lead=()):
    rows = value.shape[0]
    for c in range(D_TILES):
        ref[lead + (pl.ds(c, rows, stride=D_TILES), slice(None))] = value[:, c * LANES:(c + 1) * LANES]


def _from_token_tiles(ref, rows, lead=()):
    return jnp.concatenate(
        [ref[lead + (pl.ds(c, rows, stride=D_TILES), slice(None))] for c in range(D_TILES)], axis=1)


def _router_kernel(x_ref, g_ref, rhi_ref, rlo_ref, before_ref, h_ref, route_ref, route_t_ref, counts_ref, run_ref):
    step = pl.program_id(0)

    @pl.when(step == 0)
    def _():
        run_ref[...] = jnp.zeros_like(run_ref)

    x = x_ref[...]
    h = _rms(x, g_ref[...])
    _to_token_tiles(h_ref, h)
    hi, lo = _split(h)
    logits = _dot(hi, rhi_ref[...]) + _dot(hi, rlo_ref[...]) + _dot(lo, rhi_ref[...])
    lane = lax.broadcasted_iota(jnp.int32, logits.shape, 1)
    neg = jnp.float32(-jnp.inf)
    logits = jnp.where(lane < N_EXPERTS, logits, neg)
    m1 = jnp.max(logits, axis=-1, keepdims=True)
    i1 = jnp.min(jnp.where(logits == m1, lane, LANES), axis=-1, keepdims=True)
    rest = jnp.where(lane == i1, neg, logits)
    m2 = jnp.max(rest, axis=-1, keepdims=True)
    i2 = jnp.min(jnp.where(rest == m2, lane, LANES), axis=-1, keepdims=True)
    e2 = jnp.exp(m2 - m1)
    w1 = 1.0 / (1.0 + e2)
    w2 = e2 / (1.0 + e2)
    out = jnp.where(lane == ROUTE_IDX0, i1.astype(F32), 0.0)
    out = jnp.where(lane == ROUTE_IDX0 + 1, i2.astype(F32), out)
    out = jnp.where(lane == ROUTE_W0, w1, out)
    out = jnp.where(lane == ROUTE_W0 + 1, w2, out)
    onehots = [jnp.where(lane == idx + j * CHOICE_LANES, 1.0, 0.0) for j, idx in enumerate((i1, i2))]
    both = onehots[0] + onehots[1]
    seen = _dot(before_ref[...], both.astype(BF16)) + run_ref[0:1, :]
    for j in range(TOP_K):
        rank = jnp.sum(onehots[j] * seen, axis=-1, keepdims=True)
        out = jnp.where(lane == ROUTE_RANK0 + j, rank, out)
    run_ref[0:1, :] = run_ref[0:1, :] + jnp.sum(both, axis=0, keepdims=True)
    route_ref[...] = out
    route_t_ref[...] = out.T
    counts_ref[...] = run_ref[...]


def _router_call(x2, g, rhi, rlo):
    t, d = x2.shape
    tm = ROW_TILE
    before = jnp.tril(jnp.ones((tm, tm), BF16), -1)
    consts = (g, rhi, rlo, before)
    return pl.pallas_call(
        _router_kernel,
        grid=(t // tm,),
        in_specs=[_row_spec(tm, d)] + [_const_spec(c) for c in consts],
        out_specs=[_row_spec(tm * D_TILES, LANES), _row_spec(tm, LANES),
                   pl.BlockSpec((LANES, tm), lambda i: (0, i)), pl.BlockSpec((SUBLANES, LANES), lambda i: (0, 0))],
        out_shape=[jax.ShapeDtypeStruct((t * D_TILES, LANES), F32), jax.ShapeDtypeStruct((t, LANES), F32),
                   jax.ShapeDtypeStruct((LANES, t), F32), jax.ShapeDtypeStruct((SUBLANES, LANES), F32)],
        scratch_shapes=[pltpu.VMEM((SUBLANES, LANES), F32)],
        compiler_params=pltpu.CompilerParams(
            dimension_semantics=("arbitrary",), vmem_limit_bytes=VMEM_LIMIT),
        name="router",
    )(x2, *consts)


GATHER_UNROLL = 16


def _token_tile(ref, r):
    return ref.at[pl.ds(pl.multiple_of(r * D_TILES, D_TILES), D_TILES), :]


def _start_row_gather(idx_ref, src_hbm, dst, sem):
    def body(r, c):
        pltpu.make_async_copy(_token_tile(src_hbm, idx_ref[0, 0, r]), _token_tile(dst, r), sem).start()
        return c

    lax.fori_loop(0, dst.shape[0] // D_TILES, body, 0, unroll=GATHER_UNROLL)


def _wait_row_gather(src_hbm, dst, sem):
    pltpu.make_async_copy(src_hbm.at[pl.ds(0, dst.shape[0]), :], dst, sem).wait()


def _group_kernel(texp_ref, tvalid_ref, rows_ref, rows_next_ref, h_hbm, w1_ref, w3_ref, w2_ref, ys_ref,
                  xbuf, sem):
    i = pl.program_id(0)
    slot = i % 2

    @pl.when(i == 0)
    def _():
        _start_row_gather(rows_ref, h_hbm, xbuf.at[0], sem.at[0])

    @pl.when(i + 1 < pl.num_programs(0))
    def _():
        _start_row_gather(rows_next_ref, h_hbm, xbuf.at[1 - slot], sem.at[1 - slot])

    _wait_row_gather(h_hbm, xbuf.at[slot], sem.at[slot])

    @pl.when(tvalid_ref[i] == 1)
    def _():
        xb = _from_token_tiles(xbuf, GROUP_TILE, (slot,)).astype(BF16)
        _to_token_tiles(ys_ref, _swiglu_rows(xb, w1_ref, w3_ref, w2_ref, (0,)))

    @pl.when(tvalid_ref[i] == 0)
    def _():
        ys_ref[...] = jnp.zeros_like(ys_ref)


def _group_call(tile_expert, tile_valid, row_token, h3, w1, w3, w2):
    n_tiles = tile_expert.shape[0]
    tm = GROUP_TILE
    rows = row_token.reshape(n_tiles, 1, tm)

    def wspec(arr):
        return pl.BlockSpec((1,) + arr.shape[1:], lambda i, te, tv: (te[i], 0, 0))

    grid_spec = pltpu.PrefetchScalarGridSpec(
        num_scalar_prefetch=2,
        grid=(n_tiles,),
        in_specs=[
            pl.BlockSpec((1, 1, tm), lambda i, te, tv: (i, 0, 0), memory_space=pltpu.SMEM),
            pl.BlockSpec((1, 1, tm), lambda i, te, tv: (jnp.minimum(i + 1, n_tiles - 1), 0, 0),
                         memory_space=pltpu.SMEM),
            pl.BlockSpec(memory_space=pl.ANY),
            wspec(w1), wspec(w3), wspec(w2),
        ],
        out_specs=pl.BlockSpec((tm * D_TILES, LANES), lambda i, te, tv: (i, 0)),
        scratch_shapes=[pltpu.VMEM((2, tm * D_TILES, LANES), F32), pltpu.SemaphoreType.DMA((2,))],
    )
    return pl.pallas_call(
        _group_kernel,
        grid_spec=grid_spec,
        out_shape=jax.ShapeDtypeStruct((n_tiles * tm * D_TILES, LANES), F32),
        compiler_params=pltpu.CompilerParams(
            dimension_semantics=("arbitrary",), vmem_limit_bytes=VMEM_LIMIT),
        name="moe_group",
    )(tile_expert, tile_valid, rows, rows, h3, w1, w3, w2)


def _combine_kernel(p0_ref, p1_ref, p0_next_ref, p1_next_ref, ys_hbm, x_ref, route_ref, g_ref, o_ref,
                    buf, sem, *, apply_final_norm):
    i = pl.program_id(0)
    slot = i % 2

    def start(refs, s):
        for j in range(TOP_K):
            _start_row_gather(refs[j], ys_hbm, buf.at[s, j], sem.at[s, j])

    @pl.when(i == 0)
    def _():
        start((p0_ref, p1_ref), 0)

    @pl.when(i + 1 < pl.num_programs(0))
    def _():
        start((p0_next_ref, p1_next_ref), 1 - slot)

    for j in range(TOP_K):
        _wait_row_gather(ys_hbm, buf.at[slot, j], sem.at[slot, j])
    route = route_ref[...]
    y = x_ref[...]
    for j in range(TOP_K):
        y = y + route[:, ROUTE_W0 + j:ROUTE_W0 + j + 1] * _from_token_tiles(buf, ROW_TILE, (slot, j))
    o_ref[...] = _rms(y, g_ref[...]) if apply_final_norm else y


def _combine_call(pos0, pos1, ys, x2, route, g, apply_final_norm):
    t, d = x2.shape
    tm = ROW_TILE
    n = t // tm

    def ispec(shift):
        return pl.BlockSpec((1, 1, tm), lambda i: (jnp.minimum(i + shift, n - 1), 0, 0),
                            memory_space=pltpu.SMEM)

    p0 = pos0.reshape(n, 1, tm)
    p1 = pos1.reshape(n, 1, tm)
    return pl.pallas_call(
        functools.partial(_combine_kernel, apply_final_norm=apply_final_norm),
        grid=(n,),
        in_specs=[ispec(0), ispec(0), ispec(1), ispec(1), pl.BlockSpec(memory_space=pl.ANY),
                  _row_spec(tm, d), _row_spec(tm, LANES), _const_spec(g)],
        out_specs=_row_spec(tm, d),
        out_shape=jax.ShapeDtypeStruct((t, d), F32),
        scratch_shapes=[pltpu.VMEM((2, TOP_K, tm * D_TILES, LANES), F32),
                        pltpu.SemaphoreType.DMA((2, TOP_K))],
        compiler_params=pltpu.CompilerParams(
            dimension_semantics=("arbitrary",), vmem_limit_bytes=VMEM_LIMIT),
        name="moe_combine",
    )(p0, p1, p0, p1, ys, x2, route, g)


def _final_norm_kernel(x_ref, g_ref, o_ref):
    o_ref[...] = _rms(x_ref[...], g_ref[...])


def _final_norm_call(x2, g):
    t, d = x2.shape
    tm = ROW_TILE
    return pl.pallas_call(
        _final_norm_kernel,
        grid=(t // tm,),
        in_specs=[_row_spec(tm, d), _const_spec(g)],
        out_specs=_row_spec(tm, d),
        out_shape=jax.ShapeDtypeStruct((t, d), F32),
        name="final_norm",
    )(x2, g)


def _routing_plan(route_t, choice_counts, n_tiles):
    t = route_t.shape[1]
    tm = GROUP_TILE
    choice = [route_t[ROUTE_IDX0 + j].astype(jnp.int32) for j in range(TOP_K)]
    rank = [route_t[ROUTE_RANK0 + j].astype(jnp.int32) for j in range(TOP_K)]
    per_choice = jnp.stack([choice_counts[0, j * CHOICE_LANES:j * CHOICE_LANES + N_EXPERTS]
                            for j in range(TOP_K)]).astype(jnp.int32)
    counts = jnp.sum(per_choice, axis=0)
    tiles_per = (counts + tm - 1) // tm
    tile_end = jnp.cumsum(tiles_per)
    tile_start = tile_end - tiles_per
    earlier = jnp.cumsum(per_choice, axis=0) - per_choice
    pos = [tile_start[choice[j]] * tm + earlier[j][choice[j]] + rank[j] for j in range(TOP_K)]
    experts = jnp.concatenate(choice)
    tile_ids = jnp.arange(n_tiles, dtype=jnp.int32)
    tile_expert = jnp.minimum(
        jnp.sum((tile_ids[:, None] >= tile_end[None, :]).astype(jnp.int32), axis=1), N_EXPERTS - 1)
    tile_valid = (tile_ids < tile_end[-1]).astype(jnp.int32)
    order = jnp.argsort(experts, stable=True).astype(jnp.int32)
    row_expert = jnp.repeat(tile_expert, tm)
    row_rank = jnp.arange(n_tiles * tm, dtype=jnp.int32) - tile_start[row_expert] * tm
    row_used = (row_rank < counts[row_expert]) & (jnp.repeat(tile_valid, tm) == 1)
    src = jnp.clip((jnp.cumsum(counts) - counts)[row_expert] + row_rank, 0, t * TOP_K - 1)
    row_token = jnp.where(row_used, order[src] % t, 0)
    last_expert = tile_expert[jnp.maximum(tile_end[-1] - 1, 0)]
    tile_expert = jnp.where(tile_valid == 1, tile_expert, last_expert)
    return tile_expert, tile_valid, row_token, pos[0], pos[1]


def _pack_mixer_weights(w_in):
    glr = jnp.pad(w_in[:, OFF_GLR:GATE_OFF], ((0, 0), (0, LANES - GLA_RANK)))
    return jnp.concatenate([w_in[:, :OFF_GLR], glr], axis=1).astype(BF16)


def _row(v):
    return v.reshape(1, -1).astype(F32)


def kernel(x, norm_mix, w_in, b_gate, sgu_w, sgu_b, sgu_norm, conv_w, conv_b, pool_w, pool_scale, gla_wg2, gla_bg, gla_norm, branch_proj, w_out, norm_ffn, ffn_w1, ffn_w3, ffn_w2, moe_router, moe_w1, moe_w3, moe_w2, final_norm):
    b, s, d = x.shape
    t = b * s
    depth = norm_mix.shape[0]
    causal = jnp.tril(jnp.ones((SGU_CHUNK, SGU_CHUNK), dtype=bool))
    out = None
    for l in range(depth):
        w_mix = _pack_mixer_weights(w_in[l])
        sguw = jnp.concatenate([jnp.where(causal, sgu_w[l, hh], 0.0) for hh in range(SGU_HEADS)],
                               axis=0).astype(BF16)
        sgub = jnp.repeat(sgu_b[l].T, SGU_HEAD_DIM, axis=1).astype(F32)
        pw = jax.scipy.linalg.block_diag(*[pool_w[l, gi] for gi in range(len(POOL_WINDOWS))]).astype(BF16)
        wg2 = jnp.pad(gla_wg2[l], ((0, LANES - GLA_RANK), (0, 0))).astype(BF16)
        gn = jnp.tile(gla_norm[l], GLA_HEADS)
        i = l // 2
        slabs = {}
        if l == 0:
            slabs["branch_proj"] = branch_proj.reshape(-1, d)
            slabs["w_out"] = w_out.reshape(-1, d)
        if l % 2 == 0:
            slabs["ffn_w1"] = ffn_w1[i]
            slabs["ffn_w3"] = ffn_w3[i]
            if l + 1 < depth:
                slabs["moe_w1"] = moe_w1[(l + 1) // 2].reshape(-1, D_FF)
                slabs["moe_w3"] = moe_w3[(l + 1) // 2].reshape(-1, D_FF)
        else:
            slabs["moe_w2"] = moe_w2[i].reshape(-1, d)
        y, *cast = _mixer_call(x, _row(norm_mix[l]), w_mix, sguw, sgub, _row(sgu_norm[l]),
                               conv_w[l].astype(F32), _row(conv_b[l]), pw, _row(pool_scale[l]),
                               wg2, _row(gla_bg[l]), _row(gn), tuple(slabs.values()))
        cast = dict(zip(slabs, cast))
        if l == 0:
            proj_b = cast["branch_proj"].reshape(depth, N_BRANCH * BRANCH_WIDTH, d)
            w_out_b = cast["w_out"].reshape(depth, d, d)
        if "moe_w1" in cast:
            moe_up = [cast[k].reshape(N_EXPERTS, d, D_FF) for k in ("moe_w1", "moe_w3")]
        if "moe_w2" in cast:
            moe_down = cast["moe_w2"].reshape(N_EXPERTS, D_FF, d)
        x2 = _merge_call(x.reshape(t, d), y.reshape(t, N_BRANCH * BRANCH_WIDTH), _row(norm_mix[l]),
                         w_in[l][:, GATE_OFF:].astype(BF16), _row(b_gate[l]), proj_b[l], w_out_b[l])
        if l % 2 == 0:
            x2 = _ffn_call(x2, _row(norm_ffn[l]), cast["ffn_w1"], cast["ffn_w3"], ffn_w2[i].astype(BF16))
            if l == depth - 1:
                out = _final_norm_call(x2, _row(final_norm))
        else:
            rw = jnp.pad(moe_router[i], ((0, 0), (0, LANES - N_EXPERTS)))
            rhi = rw.astype(BF16)
            h2, route, route_t, choice_counts = _router_call(
                x2, _row(norm_ffn[l]), rhi, (rw - rhi.astype(F32)).astype(BF16))
            n_tiles = (t * TOP_K) // GROUP_TILE + N_EXPERTS
            tile_expert, tile_valid, row_token, pos0, pos1 = _routing_plan(route_t, choice_counts, n_tiles)
            ys = _group_call(tile_expert, tile_valid, row_token, h2, moe_up[0], moe_up[1], moe_down)
            x2 = _combine_call(pos0, pos1, ys, x2, route, _row(final_norm), l == depth - 1)
            if l == depth - 1:
                out = x2
        x = x2.reshape(b, s, d)
    return out.reshape(b, s, d)
```

```python
import functools

import numpy as np
import jax
import jax.numpy as jnp
from jax import lax
from jax.experimental import pallas as pl
from jax.experimental.pallas import tpu as pltpu

F32 = jnp.float32
BF16 = jnp.bfloat16

D_MODEL = 1024
N_BRANCH = 4
BRANCH_WIDTH = 256
SGU_HEADS = 4
SGU_HEAD_DIM = BRANCH_WIDTH // SGU_HEADS
SGU_CHUNK = 128
CONV_WIDTH = 3
POOL_WINDOWS = (2, 4, 8, 16)
POOL_GROUP_DIM = BRANCH_WIDTH // len(POOL_WINDOWS)
POOL_HISTORY = 16
POOL_PAD = 8
assert POOL_WINDOWS == (2, 4, 8, 16)
CONV_HISTORY = 8
GLA_HEADS = 4
GLA_DK = 32
GLA_DV = BRANCH_WIDTH // GLA_HEADS
GLA_HK = GLA_HEADS * GLA_DK
GLA_RANK = 16
GLA_TAU = 16.0
GLA_CHUNK = 64
D_FF = 2816
N_EXPERTS = 8
TOP_K = 2
EPS = 1e-6

A_COLS = 2 * BRANCH_WIDTH
B_COLS = 3 * BRANCH_WIDTH
C_COLS = BRANCH_WIDTH
QKVR_COLS = 2 * GLA_HK + 2 * BRANCH_WIDTH
LANES = 128
BF16_SUBLANES = 16
OFF_A = 0
OFF_B = OFF_A + A_COLS
OFF_C = OFF_B + B_COLS
OFF_D = OFF_C + C_COLS
OFF_GLR = OFF_D + QKVR_COLS
MIX_COLS = OFF_GLR + LANES
GATE_OFF = OFF_GLR + GLA_RANK

MIX_TS = 512
ROW_TILE = 512
DENSE_TILE = 1024
GROUP_TILE = 512
FF_CHUNK = 256
VMEM_LIMIT = 56 * 1024 * 1024


def _dot(a, b):
    return jnp.dot(a, b, preferred_element_type=F32)


def _dot_nt(a, b):
    return lax.dot_general(a, b, (((1,), (1,)), ((), ())), preferred_element_type=F32)


def _dot_tn(a, b):
    return lax.dot_general(a, b, (((0,), (0,)), ((), ())), preferred_element_type=F32)


def _split(x):
    hi = x.astype(BF16)
    lo = (x - hi.astype(F32)).astype(BF16)
    return hi, lo


def _rms(x, g):
    ms = jnp.mean(x * x, axis=-1, keepdims=True)
    return x * lax.rsqrt(ms + EPS) * g


def _sigmoid(x):
    return 1.0 / (1.0 + jnp.exp(-x))


def _silu(x):
    return x * _sigmoid(x)


def _gelu_tanh(x):
    c = np.float32(np.sqrt(2.0 / np.pi))
    return x * (0.5 * (1.0 + jnp.tanh(c * (x + 0.044715 * (x * x * x)))))


def _group_mean_matrix(n, group):
    r = lax.broadcasted_iota(jnp.int32, (n, n), 0) // group
    c = lax.broadcasted_iota(jnp.int32, (n, n), 1) // group
    return jnp.where(r == c, 1.0 / group, 0.0).astype(BF16)


def _group_rms(x, bd, g):
    hi, lo = _split(x * x)
    ms = _dot(hi, bd) + _dot(lo, bd)
    return x * lax.rsqrt(ms + EPS) * g


def _mixer_kernel(x_ref, g_ref, w_ref, sguw_ref, sgub_ref, sgun_ref, cw_ref, cb_ref, pw_ref, ps_ref,
                  wg2_ref, bg_ref, gn_ref, *rest, n_cast):
    cast_in = rest[:n_cast]
    y_ref = rest[n_cast]
    cast_out = rest[n_cast + 1:2 * n_cast + 1]
    state_ref, ybuf_ref, zbuf_ref = rest[2 * n_cast + 1:]
    for src, dst in zip(cast_in, cast_out):
        dst[...] = src[...].astype(BF16)
    si = pl.program_id(1)
    ts = x_ref.shape[1]
    x = x_ref[0]
    h = _rms(x, g_ref[...]).astype(BF16)

    @pl.when(si == 0)
    def _():
        state_ref[...] = jnp.zeros_like(state_ref)
        ybuf_ref[0:CONV_HISTORY, :] = jnp.zeros((CONV_HISTORY, BRANCH_WIDTH), F32)
        zbuf_ref[:, 0:POOL_PAD + POOL_HISTORY, :] = jnp.zeros((3, POOL_PAD + POOL_HISTORY, BRANCH_WIDTH), F32)

    bd64 = _group_mean_matrix(BRANCH_WIDTH, SGU_HEAD_DIM)
    lane = lax.broadcasted_iota(jnp.int32, (1, BRANCH_WIDTH), 1)

    z = _gelu_tanh(_dot(h, w_ref[:, OFF_A:OFF_A + A_COLS]))
    u = z[:, :BRANCH_WIDTH]
    vn = _group_rms(z[:, BRANCH_WIDTH:], bd64, sgun_ref[...])
    lane_head = lane // SGU_HEAD_DIM
    vnb = vn.astype(BF16)
    for c in range(ts // SGU_CHUNK):
        rows = slice(c * SGU_CHUNK, (c + 1) * SGU_CHUNK)
        stacked = _dot(sguw_ref[...], vnb[rows])
        mixed = stacked[(SGU_HEADS - 1) * SGU_CHUNK:]
        for hh in range(SGU_HEADS - 2, -1, -1):
            mixed = jnp.where(lane_head == hh, stacked[hh * SGU_CHUNK:(hh + 1) * SGU_CHUNK], mixed)
        y_ref[0, rows, 0:BRANCH_WIDTH] = (u[rows] * (mixed + sgub_ref[...])).astype(BF16)

    zb = _dot(h, w_ref[:, OFF_B:OFF_B + B_COLS])
    gate_b = zb[:, BRANCH_WIDTH:2 * BRANCH_WIDTH]
    yv = zb[:, 2 * BRANCH_WIDTH:] * zb[:, :BRANCH_WIDTH]
    ybuf_ref[CONV_HISTORY:CONV_HISTORY + ts, :] = yv
    conv = cb_ref[...] + yv * cw_ref[CONV_WIDTH - 1:CONV_WIDTH, :]
    for i in range(CONV_WIDTH - 1):
        back = CONV_WIDTH - 1 - i
        conv = conv + ybuf_ref[CONV_HISTORY - back:CONV_HISTORY - back + ts, :] * cw_ref[i:i + 1, :]
    ybuf_ref[0:CONV_HISTORY, :] = ybuf_ref[ts:ts + CONV_HISTORY, :]
    y_ref[0, :, BRANCH_WIDTH:2 * BRANCH_WIDTH] = (gate_b * conv).astype(BF16)

    zc = _dot(h, w_ref[:, OFF_C:OFF_C + C_COLS])
    ext = POOL_HISTORY + ts
    p0 = POOL_PAD
    zbuf_ref[0, p0 + POOL_HISTORY:p0 + ext, :] = zc
    s2 = zbuf_ref[0, p0:p0 + ext, :] + zbuf_ref[0, p0 - 1:p0 - 1 + ext, :]
    zbuf_ref[1, p0:p0 + ext, :] = s2
    s4 = s2 + zbuf_ref[1, p0 - 2:p0 - 2 + ext, :]
    zbuf_ref[2, p0:p0 + ext, :] = s4
    s8 = s4 + zbuf_ref[2, p0 - 4:p0 - 4 + ext, :]
    sums = {2: s2[POOL_HISTORY:], 4: s4[POOL_HISTORY:], 8: s8[POOL_HISTORY:],
            16: s8[POOL_HISTORY:] + s8[POOL_HISTORY - 8:POOL_HISTORY - 8 + ts]}
    zbuf_ref[0, p0:p0 + POOL_HISTORY, :] = zbuf_ref[0, p0 + ts:p0 + ext, :]
    lane_group = lane // POOL_GROUP_DIM
    wsum = sums[POOL_WINDOWS[-1]]
    win = jnp.full((1, BRANCH_WIDTH), POOL_WINDOWS[-1], jnp.int32)
    for gi in range(len(POOL_WINDOWS) - 2, -1, -1):
        wsum = jnp.where(lane_group == gi, sums[POOL_WINDOWS[gi]], wsum)
        win = jnp.where(lane_group == gi, POOL_WINDOWS[gi], win)
    tpos = si * ts + lax.broadcasted_iota(jnp.int32, (ts, 1), 0)
    count = jnp.minimum(tpos + 1, win).astype(F32)
    pooled = wsum / count - zc
    y_ref[0, :, 2 * BRANCH_WIDTH:3 * BRANCH_WIDTH] = (
        _dot(pooled.astype(BF16), pw_ref[...]) * ps_ref[...]).astype(BF16)

    zd = _dot(h, w_ref[:, OFF_D:OFF_D + QKVR_COLS + LANES])
    q = zd[:, 0:GLA_HK]
    k = zd[:, GLA_HK:2 * GLA_HK]
    v = zd[:, 2 * GLA_HK:2 * GLA_HK + BRANCH_WIDTH]
    r = zd[:, 2 * GLA_HK + BRANCH_WIDTH:QKVR_COLS]
    glr = zd[:, QKVR_COLS:]
    a = _dot(glr.astype(BF16), wg2_ref[...]) + bg_ref[...]
    log_g = (jnp.minimum(a, 0.0) - jnp.log(1.0 + jnp.exp(-jnp.abs(a)))) * (1.0 / GLA_TAU)
    blk = 4 * GLA_CHUNK
    rr = lax.broadcasted_iota(jnp.int32, (blk, blk), 0)
    cc = lax.broadcasted_iota(jnp.int32, (blk, blk), 1)
    same = (rr // GLA_CHUNK) == (cc // GLA_CHUNK)
    tri = jnp.where(same & (cc <= rr), 1.0, 0.0).astype(BF16)
    ones = jnp.where(same, 1.0, 0.0).astype(BF16)
    cums, lasts = [], []
    for b in range(ts // blk):
        hi, lo = _split(log_g[b * blk:(b + 1) * blk])
        cums.append(_dot(tri, hi) + _dot(tri, lo))
        lasts.append(_dot(ones, hi) + _dot(ones, lo))
    cum = jnp.concatenate(cums, axis=0)
    last = jnp.concatenate(lasts, axis=0)
    qd_all = (q * (GLA_DK ** -0.5) * jnp.exp(cum)).astype(BF16)
    ki_all = (k * jnp.exp(-cum)).astype(BF16)
    ke_all = (k * jnp.exp(last - cum)).astype(BF16)
    v_all = v.astype(BF16)
    decay_all = jnp.exp(last)

    nrow = GLA_HEADS * GLA_CHUNK
    row_head = lax.broadcasted_iota(jnp.int32, (nrow, 1), 0) // GLA_CHUNK
    mask_k = jnp.where(row_head == lax.broadcasted_iota(jnp.int32, (nrow, GLA_HK), 1) // GLA_DK,
                       1.0, 0.0).astype(BF16)
    mask_v = jnp.where(row_head == lax.broadcasted_iota(jnp.int32, (nrow, BRANCH_WIDTH), 1) // GLA_DV,
                       1.0, 0.0).astype(BF16)
    causal = (lax.broadcasted_iota(jnp.int32, (GLA_CHUNK, nrow), 1) % GLA_CHUNK
              <= lax.broadcasted_iota(jnp.int32, (GLA_CHUNK, nrow), 0))
    mask_s = (lax.broadcasted_iota(jnp.int32, (BRANCH_WIDTH, GLA_HK), 0) // GLA_DV
              == lax.broadcasted_iota(jnp.int32, (BRANCH_WIDTH, GLA_HK), 1) // GLA_DK)

    state = state_ref[...]
    outs = []
    for n in range(ts // GLA_CHUNK):
        rows = slice(n * GLA_CHUNK, (n + 1) * GLA_CHUNK)
        qd = qd_all[rows]
        vc = v_all[rows]
        kbd = jnp.concatenate([ki_all[rows]] * GLA_HEADS, axis=0) * mask_k
        vbd = jnp.concatenate([vc] * GLA_HEADS, axis=0) * mask_v
        scores = jnp.where(causal, _dot_nt(qd, kbd), 0.0)
        outs.append(_dot(scores.astype(BF16), vbd) + _dot_nt(qd, state.astype(BF16)))
        kv = _dot_tn(vc, ke_all[rows])
        state = state * decay_all[n * GLA_CHUNK:n * GLA_CHUNK + 1, :] + jnp.where(mask_s, kv, 0.0)
    state_ref[...] = state
    on = _group_rms(jnp.concatenate(outs, axis=0), bd64, gn_ref[...])
    y_ref[0, :, 3 * BRANCH_WIDTH:] = (_silu(r) * on).astype(BF16)


def _mixer_call(x, g, w_mix, sguw, sgub, sgun, cw, cb, pw, ps, wg2, bg, gn, cast_slabs=()):
    b, s, d = x.shape
    ts = MIX_TS
    n_si = s // ts
    steps = b * n_si

    def const(arr):
        return pl.BlockSpec(arr.shape, lambda bi, si: (0,) * arr.ndim)

    def slab(arr):
        rows = arr.shape[0]
        per = next(r for r in range(BF16_SUBLANES, rows + 1, BF16_SUBLANES)
                   if rows % r == 0 and r * steps >= rows)
        last = rows // per - 1
        return pl.BlockSpec((per, arr.shape[1]), lambda bi, si: (jnp.minimum(bi * n_si + si, last), 0))

    consts = (g, w_mix, sguw, sgub, sgun, cw, cb, pw, ps, wg2, bg, gn)
    return pl.pallas_call(
        functools.partial(_mixer_kernel, n_cast=len(cast_slabs)),
        grid=(b, n_si),
        in_specs=([pl.BlockSpec((1, ts, d), lambda bi, si: (bi, si, 0))] + [const(c) for c in consts]
                  + [slab(a) for a in cast_slabs]),
        out_specs=([pl.BlockSpec((1, ts, N_BRANCH * BRANCH_WIDTH), lambda bi, si: (bi, si, 0))]
                   + [slab(a) for a in cast_slabs]),
        out_shape=([jax.ShapeDtypeStruct((b, s, N_BRANCH * BRANCH_WIDTH), BF16)]
                   + [jax.ShapeDtypeStruct(a.shape, BF16) for a in cast_slabs]),
        scratch_shapes=[
            pltpu.VMEM((BRANCH_WIDTH, GLA_HK), F32),
            pltpu.VMEM((CONV_HISTORY + ts, BRANCH_WIDTH), F32),
            pltpu.VMEM((3, POOL_PAD + POOL_HISTORY + ts, BRANCH_WIDTH), F32),
        ],
        compiler_params=pltpu.CompilerParams(
            dimension_semantics=("arbitrary", "arbitrary"), vmem_limit_bytes=VMEM_LIMIT),
        name="mixer",
    )(x, *consts, *cast_slabs)


def _merge_kernel(x_ref, y_ref, g_ref, wg_ref, bgate_ref, p_ref, wo_ref, o_ref):
    x = x_ref[...]
    h = _rms(x, g_ref[...]).astype(BF16)
    merged = None
    for i in range(N_BRANCH):
        cols = slice(i * D_MODEL, (i + 1) * D_MODEL)
        gate = _sigmoid(_dot(h, wg_ref[:, cols]) + bgate_ref[:, cols])
        rows = slice(i * BRANCH_WIDTH, (i + 1) * BRANCH_WIDTH)
        term = gate * _dot(y_ref[:, rows], p_ref[rows, :])
        merged = term if merged is None else merged + term
    o_ref[...] = x + _dot(merged.astype(BF16), wo_ref[...])


def _row_spec(tm, width):
    return pl.BlockSpec((tm, width), lambda i: (i, 0))


def _const_spec(arr):
    return pl.BlockSpec(arr.shape, lambda i: (0,) * arr.ndim, pipeline_mode=pl.Buffered(1))


def _merge_call(x2, y2, g, wg, bgate, p, wo):
    t, d = x2.shape
    tm = DENSE_TILE
    consts = (g, wg, bgate, p, wo)
    return pl.pallas_call(
        _merge_kernel,
        grid=(t // tm,),
        in_specs=[_row_spec(tm, d), _row_spec(tm, y2.shape[1])] + [_const_spec(c) for c in consts],
        out_specs=_row_spec(tm, d),
        out_shape=jax.ShapeDtypeStruct((t, d), F32),
        compiler_params=pltpu.CompilerParams(
            dimension_semantics=("arbitrary",), vmem_limit_bytes=VMEM_LIMIT),
        name="merge",
    )(x2, y2, *consts)


FF_CHUNKS = D_FF // FF_CHUNK


def _swiglu_rows(hb, w1_ref, w3_ref, w2_ref, lead):
    acc = None
    for f in range(FF_CHUNKS):
        cols = slice(f * FF_CHUNK, (f + 1) * FF_CHUNK)
        up = _dot(hb, w1_ref[lead + (slice(None), cols)])
        gt = _dot(hb, w3_ref[lead + (slice(None), cols)])
        part = _dot((_silu(up) * gt).astype(BF16), w2_ref[lead + (cols, slice(None))])
        acc = part if acc is None else acc + part
    return acc


def _ffn_kernel(x_ref, g_ref, w1_ref, w3_ref, w2_ref, o_ref):
    x = x_ref[...]
    hb = _rms(x, g_ref[...]).astype(BF16)
    o_ref[...] = x + _swiglu_rows(hb, w1_ref, w3_ref, w2_ref, ())


def _ffn_call(x2, g, w1, w3, w2):
    t, d = x2.shape
    tm = DENSE_TILE
    consts = (g, w1, w3, w2)
    return pl.pallas_call(
        _ffn_kernel,
        grid=(t // tm,),
        in_specs=[_row_spec(tm, d)] + [_const_spec(c) for c in consts],
        out_specs=_row_spec(tm, d),
        out_shape=jax.ShapeDtypeStruct((t, d), F32),
        compiler_params=pltpu.CompilerParams(
            dimension_semantics=("arbitrary",), vmem_limit_bytes=VMEM_LIMIT),
        name="ffn",
    )(x2, *consts)


ROUTE_IDX0 = N_EXPERTS
ROUTE_W0 = N_EXPERTS + TOP_K
ROUTE_RANK0 = N_EXPERTS + 2 * TOP_K
CHOICE_LANES = LANES // TOP_K
SUBLANES = 8
D_TILES = D_MODEL // LANES


def _to_token_tiles(ref, value, lead=()):
    rows = value.shape[0]
    for c in range(D_TILES):
        ref[lead + (pl.ds(c, rows, stride=D_TILES), slice(None))] = value[:, c * LANES:(c + 1) * LANES]


def _from_token_tiles(ref, rows, lead=()):
    return jnp.concatenate(
        [ref[lead + (pl.ds(c, rows, stride=D_TILES), slice(None))] for c in range(D_TILES)], axis=1)


def _router_kernel(x_ref, g_ref, rhi_ref, rlo_ref, before_ref, h_ref, route_ref, route_t_ref, counts_ref, run_ref):
    step = pl.program_id(0)

    @pl.when(step == 0)
    def _():
        run_ref[...] = jnp.zeros_like(run_ref)

    x = x_ref[...]
    h = _rms(x, g_ref[...])
    _to_token_tiles(h_ref, h)
    hi, lo = _split(h)
    logits = _dot(hi, rhi_ref[...]) + _dot(hi, rlo_ref[...]) + _dot(lo, rhi_ref[...])
    lane = lax.broadcasted_iota(jnp.int32, logits.shape, 1)
    neg = jnp.float32(-jnp.inf)
    logits = jnp.where(lane < N_EXPERTS, logits, neg)
    m1 = jnp.max(logits, axis=-1, keepdims=True)
    i1 = jnp.min(jnp.where(logits == m1, lane, LANES), axis=-1, keepdims=True)
    rest = jnp.where(lane == i1, neg, logits)
    m2 = jnp.max(rest, axis=-1, keepdims=True)
    i2 = jnp.min(jnp.where(rest == m2, lane, LANES), axis=-1, keepdims=True)
    e2 = jnp.exp(m2 - m1)
    w1 = 1.0 / (1.0 + e2)
    w2 = e2 / (1.0 + e2)
    out = jnp.where(lane == ROUTE_IDX0, i1.astype(F32), 0.0)
    out = jnp.where(lane == ROUTE_IDX0 + 1, i2.astype(F32), out)
    out = jnp.where(lane == ROUTE_W0, w1, out)
    out = jnp.where(lane == ROUTE_W0 + 1, w2, out)
    onehots = [jnp.where(lane == idx + j * CHOICE_LANES, 1.0, 0.0) for j, idx in enumerate((i1, i2))]
    both = onehots[0] + onehots[1]
    seen = _dot(before_ref[...], both.astype(BF16)) + run_ref[0:1, :]
    for j in range(TOP_K):
        rank = jnp.sum(onehots[j] * seen, axis=-1, keepdims=True)
        out = jnp.where(lane == ROUTE_RANK0 + j, rank, out)
    run_ref[0:1, :] = run_ref[0:1, :] + jnp.sum(both, axis=0, keepdims=True)
    route_ref[...] = out
    route_t_ref[...] = out.T
    counts_ref[...] = run_ref[...]


def _router_call(x2, g, rhi, rlo):
    t, d = x2.shape
    tm = ROW_TILE
    before = jnp.tril(jnp.ones((tm, tm), BF16), -1)
    consts = (g, rhi, rlo, before)
    return pl.pallas_call(
        _router_kernel,
        grid=(t // tm,),
        in_specs=[_row_spec(tm, d)] + [_const_spec(c) for c in consts],
        out_specs=[_row_spec(tm * D_TILES, LANES), _row_spec(tm, LANES),
                   pl.BlockSpec((LANES, tm), lambda i: (0, i)), pl.BlockSpec((SUBLANES, LANES), lambda i: (0, 0))],
        out_shape=[jax.ShapeDtypeStruct((t * D_TILES, LANES), F32), jax.ShapeDtypeStruct((t, LANES), F32),
                   jax.ShapeDtypeStruct((LANES, t), F32), jax.ShapeDtypeStruct((SUBLANES, LANES), F32)],
        scratch_shapes=[pltpu.VMEM((SUBLANES, LANES), F32)],
        compiler_params=pltpu.CompilerParams(
            dimension_semantics=("arbitrary",), vmem_limit_bytes=VMEM_LIMIT),
        name="router",
    )(x2, *consts)


GATHER_UNROLL = 16


def _token_tile(ref, r):
    return ref.at[pl.ds(pl.multiple_of(r * D_TILES, D_TILES), D_TILES), :]


def _start_row_gather(idx_ref, src_hbm, dst, sem):
    def body(r, c):
        pltpu.make_async_copy(_token_tile(src_hbm, idx_ref[0, 0, r]), _token_tile(dst, r), sem).start()
        return c

    lax.fori_loop(0, dst.shape[0] // D_TILES, body, 0, unroll=GATHER_UNROLL)


def _wait_row_gather(src_hbm, dst, sem):
    pltpu.make_async_copy(src_hbm.at[pl.ds(0, dst.shape[0]), :], dst, sem).wait()


def _group_kernel(texp_ref, tvalid_ref, rows_ref, rows_next_ref, h_hbm, w1_ref, w3_ref, w2_ref, ys_ref,
                  xbuf, sem):
    i = pl.program_id(0)
    slot = i % 2

    @pl.when(i == 0)
    def _():
        _start_row_gather(rows_ref, h_hbm, xbuf.at[0], sem.at[0])

    @pl.when(i + 1 < pl.num_programs(0))
    def _():
        _start_row_gather(rows_next_ref, h_hbm, xbuf.at[1 - slot], sem.at[1 - slot])

    _wait_row_gather(h_hbm, xbuf.at[slot], sem.at[slot])

    @pl.when(tvalid_ref[i] == 1)
    def _():
        xb = _from_token_tiles(xbuf, GROUP_TILE, (slot,)).astype(BF16)
        _to_token_tiles(ys_ref, _swiglu_rows(xb, w1_ref, w3_ref, w2_ref, (0,)))

    @pl.when(tvalid_ref[i] == 0)
    def _():
        ys_ref[...] = jnp.zeros_like(ys_ref)


def _group_call(tile_expert, tile_valid, row_token, h3, w1, w3, w2):
    n_tiles = tile_expert.shape[0]
    tm = GROUP_TILE
    rows = row_token.reshape(n_tiles, 1, tm)

    def wspec(arr):
        return pl.BlockSpec((1,) + arr.shape[1:], lambda i, te, tv: (te[i], 0, 0))

    grid_spec = pltpu.PrefetchScalarGridSpec(
        num_scalar_prefetch=2,
        grid=(n_tiles,),
        in_specs=[
            pl.BlockSpec((1, 1, tm), lambda i, te, tv: (i, 0, 0), memory_space=pltpu.SMEM),
            pl.BlockSpec((1, 1, tm), lambda i, te, tv: (jnp.minimum(i + 1, n_tiles - 1), 0, 0),
                         memory_space=pltpu.SMEM),
            pl.BlockSpec(memory_space=pl.ANY),
            wspec(w1), wspec(w3), wspec(w2),
        ],
        out_specs=pl.BlockSpec((tm * D_TILES, LANES), lambda i, te, tv: (i, 0)),
        scratch_shapes=[pltpu.VMEM((2, tm * D_TILES, LANES), F32), pltpu.SemaphoreType.DMA((2,))],
    )
    return pl.pallas_call(
        _group_kernel,
        grid_spec=grid_spec,
        out_shape=jax.ShapeDtypeStruct((n_tiles * tm * D_TILES, LANES), F32),
        compiler_params=pltpu.CompilerParams(
            dimension_semantics=("arbitrary",), vmem_limit_bytes=VMEM_LIMIT),
        name="moe_group",
    )(tile_expert, tile_valid, rows, rows, h3, w1, w3, w2)


def _combine_kernel(p0_ref, p1_ref, p0_next_ref, p1_next_ref, ys_hbm, x_ref, route_ref, g_ref, o_ref,
                    buf, sem, *, apply_final_norm):
    i = pl.program_id(0)
    slot = i % 2

    def start(refs, s):
        for j in range(TOP_K):
            _start_row_gather(refs[j], ys_hbm, buf.at[s, j], sem.at[s, j])

    @pl.when(i == 0)
    def _():
        start((p0_ref, p1_ref), 0)

    @pl.when(i + 1 < pl.num_programs(0))
    def _():
        start((p0_next_ref, p1_next_ref), 1 - slot)

    for j in range(TOP_K):
        _wait_row_gather(ys_hbm, buf.at[slot, j], sem.at[slot, j])
    route = route_ref[...]
    y = x_ref[...]
    for j in range(TOP_K):
        y = y + route[:, ROUTE_W0 + j:ROUTE_W0 + j + 1] * _from_token_tiles(buf, ROW_TILE, (slot, j))
    o_ref[...] = _rms(y, g_ref[...]) if apply_final_norm else y


def _combine_call(pos0, pos1, ys, x2, route, g, apply_final_norm):
    t, d = x2.shape
    tm = ROW_TILE
    n = t // tm

    def ispec(shift):
        return pl.BlockSpec((1, 1, tm), lambda i: (jnp.minimum(i + shift, n - 1), 0, 0),
                            memory_space=pltpu.SMEM)

    p0 = pos0.reshape(n, 1, tm)
    p1 = pos1.reshape(n, 1, tm)
    return pl.pallas_call(
        functools.partial(_combine_kernel, apply_final_norm=apply_final_norm),
        grid=(n,),
        in_specs=[ispec(0), ispec(0), ispec(1), ispec(1), pl.BlockSpec(memory_space=pl.ANY),
                  _row_spec(tm, d), _row_spec(tm, LANES), _const_spec(g)],
        out_specs=_row_spec(tm, d),
        out_shape=jax.ShapeDtypeStruct((t, d), F32),
        scratch_shapes=[pltpu.VMEM((2, TOP_K, tm * D_TILES, LANES), F32),
                        pltpu.SemaphoreType.DMA((2, TOP_K))],
        compiler_params=pltpu.CompilerParams(
            dimension_semantics=("arbitrary",), vmem_limit_bytes=VMEM_LIMIT),
        name="moe_combine",
    )(p0, p1, p0, p1, ys, x2, route, g)


def _final_norm_kernel(x_ref, g_ref, o_ref):
    o_ref[...] = _rms(x_ref[...], g_ref[...])


def _final_norm_call(x2, g):
    t, d = x2.shape
    tm = ROW_TILE
    return pl.pallas_call(
        _final_norm_kernel,
        grid=(t // tm,),
        in_specs=[_row_spec(tm, d), _const_spec(g)],
        out_specs=_row_spec(tm, d),
        out_shape=jax.ShapeDtypeStruct((t, d), F32),
        name="final_norm",
    )(x2, g)


def _routing_plan(route_t, choice_counts, n_tiles):
    t = route_t.shape[1]
    tm = GROUP_TILE
    choice = [route_t[ROUTE_IDX0 + j].astype(jnp.int32) for j in range(TOP_K)]
    rank = [route_t[ROUTE_RANK0 + j].astype(jnp.int32) for j in range(TOP_K)]
    per_choice = jnp.stack([choice_counts[0, j * CHOICE_LANES:j * CHOICE_LANES + N_EXPERTS]
                            for j in range(TOP_K)]).astype(jnp.int32)
    counts = jnp.sum(per_choice, axis=0)
    tiles_per = (counts + tm - 1) // tm
    tile_end = jnp.cumsum(tiles_per)
    tile_start = tile_end - tiles_per
    earlier = jnp.cumsum(per_choice, axis=0) - per_choice
    pos = [tile_start[choice[j]] * tm + earlier[j][choice[j]] + rank[j] for j in range(TOP_K)]
    experts = jnp.concatenate(choice)
    tile_ids = jnp.arange(n_tiles, dtype=jnp.int32)
    tile_expert = jnp.minimum(
        jnp.sum((tile_ids[:, None] >= tile_end[None, :]).astype(jnp.int32), axis=1), N_EXPERTS - 1)
    tile_valid = (tile_ids < tile_end[-1]).astype(jnp.int32)
    order = jnp.argsort(experts, stable=True).astype(jnp.int32)
    row_expert = jnp.repeat(tile_expert, tm)
    row_rank = jnp.arange(n_tiles * tm, dtype=jnp.int32) - tile_start[row_expert] * tm
    row_used = (row_rank < counts[row_expert]) & (jnp.repeat(tile_valid, tm) == 1)
    src = jnp.clip((jnp.cumsum(counts) - counts)[row_expert] + row_rank, 0, t * TOP_K - 1)
    row_token = jnp.where(row_used, order[src] % t, 0)
    last_expert = tile_expert[jnp.maximum(tile_end[-1] - 1, 0)]
    tile_expert = jnp.where(tile_valid == 1, tile_expert, last_expert)
    return tile_expert, tile_valid, row_token, pos[0], pos[1]


def _pack_mixer_weights(w_in):
    glr = jnp.pad(w_in[:, OFF_GLR:GATE_OFF], ((0, 0), (0, LANES - GLA_RANK)))
    return jnp.concatenate([w_in[:, :OFF_GLR], glr], axis=1).astype(BF16)


def _row(v):
    return v.reshape(1, -1).astype(F32)


def kernel(x, norm_mix, w_in, b_gate, sgu_w, sgu_b, sgu_norm, conv_w, conv_b, pool_w, pool_scale, gla_wg2, gla_bg, gla_norm, branch_proj, w_out, norm_ffn, ffn_w1, ffn_w3, ffn_w2, moe_router, moe_w1, moe_w3, moe_w2, final_norm):
    b, s, d = x.shape
    t = b * s
    depth = norm_mix.shape[0]
    causal = jnp.tril(jnp.ones((SGU_CHUNK, SGU_CHUNK), dtype=bool))
    out = None
    for l in range(depth):
        w_mix = _pack_mixer_weights(w_in[l])
        sguw = jnp.concatenate([jnp.where(causal, sgu_w[l, hh], 0.0) for hh in range(SGU_HEADS)],
                               axis=0).astype(BF16)
        sgub = jnp.repeat(sgu_b[l].T, SGU_HEAD_DIM, axis=1).astype(F32)
        pw = jax.scipy.linalg.block_diag(*[pool_w[l, gi] for gi in range(len(POOL_WINDOWS))]).astype(BF16)
        wg2 = jnp.pad(gla_wg2[l], ((0, LANES - GLA_RANK), (0, 0))).astype(BF16)
        gn = jnp.tile(gla_norm[l], GLA_HEADS)
        i = l // 2
        slabs = {}
        if l == 0:
            slabs["branch_proj"] = branch_proj.reshape(-1, d)
            slabs["w_out"] = w_out.reshape(-1, d)
        if l % 2 == 0:
            slabs["ffn_w1"] = ffn_w1[i]
            slabs["ffn_w3"] = ffn_w3[i]
            slabs["ffn_w2"] = ffn_w2[i]
            if l + 1 < depth:
                slabs["moe_w1"] = moe_w1[(l + 1) // 2].reshape(-1, D_FF)
                slabs["moe_w3"] = moe_w3[(l + 1) // 2].reshape(-1, D_FF)
        else:
            slabs["moe_w2"] = moe_w2[i].reshape(-1, d)
        y, *cast = _mixer_call(x, _row(norm_mix[l]), w_mix, sguw, sgub, _row(sgu_norm[l]),
                               conv_w[l].astype(F32), _row(conv_b[l]), pw, _row(pool_scale[l]),
                               wg2, _row(gla_bg[l]), _row(gn), tuple(slabs.values()))
        cast = dict(zip(slabs, cast))
        if l == 0:
            proj_b = cast["branch_proj"].reshape(depth, N_BRANCH * BRANCH_WIDTH, d)
            w_out_b = cast["w_out"].reshape(depth, d, d)
        if "moe_w1" in cast:
            moe_up = [cast[k].reshape(N_EXPERTS, d, D_FF) for k in ("moe_w1", "moe_w3")]
        if "moe_w2" in cast:
            moe_down = cast["moe_w2"].reshape(N_EXPERTS, D_FF, d)
        x2 = _merge_call(x.reshape(t, d), y.reshape(t, N_BRANCH * BRANCH_WIDTH), _row(norm_mix[l]),
                         w_in[l][:, GATE_OFF:].astype(BF16), _row(b_gate[l]), proj_b[l], w_out_b[l])
        if l % 2 == 0:
            x2 = _ffn_call(x2, _row(norm_ffn[l]), cast["ffn_w1"], cast["ffn_w3"], cast["ffn_w2"])
            if l == depth - 1:
                out = _final_norm_call(x2, _row(final_norm))
        else:
            rw = jnp.pad(moe_router[i], ((0, 0), (0, LANES - N_EXPERTS)))
            rhi = rw.astype(BF16)
            h2, route, route_t, choice_counts = _router_call(
                x2, _row(norm_ffn[l]), rhi, (rw - rhi.astype(F32)).astype(BF16))
            n_tiles = (t * TOP_K) // GROUP_TILE + N_EXPERTS
            tile_expert, tile_valid, row_token, pos0, pos1 = _routing_plan(route_t, choice_counts, n_tiles)
            ys = _group_call(tile_expert, tile_valid, row_token, h2, moe_up[0], moe_up[1], moe_down)
            x2 = _combine_call(pos0, pos1, ys, x2, route, _row(final_norm), l == depth - 1)
            if l == depth - 1:
                out = x2
        x = x2.reshape(b, s, d)
    return out.reshape(b, s, d)
```

```python
import functools

import numpy as np
import jax
import jax.numpy as jnp
from jax import lax
from jax.experimental import pallas as pl
from jax.experimental.pallas import tpu as pltpu

F32 = jnp.float32
BF16 = jnp.bfloat16

D_MODEL = 1024
N_BRANCH = 4
BRANCH_WIDTH = 256
SGU_HEADS = 4
SGU_HEAD_DIM = BRANCH_WIDTH // SGU_HEADS
SGU_CHUNK = 128
CONV_WIDTH = 3
POOL_WINDOWS = (2, 4, 8, 16)
POOL_GROUP_DIM = BRANCH_WIDTH // len(POOL_WINDOWS)
POOL_HISTORY = 16
POOL_PAD = 8
assert POOL_WINDOWS == (2, 4, 8, 16)
CONV_HISTORY = 8
GLA_HEADS = 4
GLA_DK = 32
GLA_DV = BRANCH_WIDTH // GLA_HEADS
GLA_HK = GLA_HEADS * GLA_DK
GLA_RANK = 16
GLA_TAU = 16.0
GLA_CHUNK = 64
D_FF = 2816
N_EXPERTS = 8
TOP_K = 2
EPS = 1e-6

A_COLS = 2 * BRANCH_WIDTH
B_COLS = 3 * BRANCH_WIDTH
C_COLS = BRANCH_WIDTH
QKVR_COLS = 2 * GLA_HK + 2 * BRANCH_WIDTH
LANES = 128
BF16_SUBLANES = 16
OFF_A = 0
OFF_B = OFF_A + A_COLS
OFF_C = OFF_B + B_COLS
OFF_D = OFF_C + C_COLS
OFF_GLR = OFF_D + QKVR_COLS
MIX_COLS = OFF_GLR + LANES
GATE_OFF = OFF_GLR + GLA_RANK

MIX_TS = 512
ROW_TILE = 512
DENSE_TILE = 1024
GROUP_TILE = 512
FF_CHUNK = 256
VMEM_LIMIT = 56 * 1024 * 1024


def _dot(a, b):
    return jnp.dot(a, b, preferred_element_type=F32)


def _dot_nt(a, b):
    return lax.dot_general(a, b, (((1,), (1,)), ((), ())), preferred_element_type=F32)


def _dot_tn(a, b):
    return lax.dot_general(a, b, (((0,), (0,)), ((), ())), preferred_element_type=F32)


def _split(x):
    hi = x.astype(BF16)
    lo = (x - hi.astype(F32)).astype(BF16)
    return hi, lo


def _rms(x, g):
    ms = jnp.mean(x * x, axis=-1, keepdims=True)
    return x * lax.rsqrt(ms + EPS) * g


def _sigmoid(x):
    return 1.0 / (1.0 + jnp.exp(-x))


def _silu(x):
    return x * _sigmoid(x)


def _gelu_tanh(x):
    c = np.float32(np.sqrt(2.0 / np.pi))
    return x * (0.5 * (1.0 + jnp.tanh(c * (x + 0.044715 * (x * x * x)))))


def _group_mean_matrix(n, group):
    r = lax.broadcasted_iota(jnp.int32, (n, n), 0) // group
    c = lax.broadcasted_iota(jnp.int32, (n, n), 1) // group
    return jnp.where(r == c, 1.0 / group, 0.0).astype(BF16)


def _group_rms(x, bd, g):
    hi, lo = _split(x * x)
    ms = _dot(hi, bd) + _dot(lo, bd)
    return x * lax.rsqrt(ms + EPS) * g


def _mixer_kernel(x_ref, g_ref, w_ref, sguw_ref, sgub_ref, sgun_ref, cw_ref, cb_ref, pw_ref, ps_ref,
                  wg2_ref, bg_ref, gn_ref, *rest, n_cast):
    cast_in = rest[:n_cast]
    y_ref = rest[n_cast]
    cast_out = rest[n_cast + 1:2 * n_cast + 1]
    state_ref, ybuf_ref, zbuf_ref = rest[2 * n_cast + 1:]
    for src, dst in zip(cast_in, cast_out):
        dst[...] = src[...].astype(BF16)
    si = pl.program_id(1)
    ts = x_ref.shape[1]
    x = x_ref[0]
    h = _rms(x, g_ref[...]).astype(BF16)

    @pl.when(si == 0)
    def _():
        state_ref[...] = jnp.zeros_like(state_ref)
        ybuf_ref[0:CONV_HISTORY, :] = jnp.zeros((CONV_HISTORY, BRANCH_WIDTH), F32)
        zbuf_ref[:, 0:POOL_PAD + POOL_HISTORY, :] = jnp.zeros((3, POOL_PAD + POOL_HISTORY, BRANCH_WIDTH), F32)

    bd64 = _group_mean_matrix(BRANCH_WIDTH, SGU_HEAD_DIM)
    lane = lax.broadcasted_iota(jnp.int32, (1, BRANCH_WIDTH), 1)

    z = _gelu_tanh(_dot(h, w_ref[:, OFF_A:OFF_A + A_COLS]))
    u = z[:, :BRANCH_WIDTH]
    vn = _group_rms(z[:, BRANCH_WIDTH:], bd64, sgun_ref[...])
    lane_head = lane // SGU_HEAD_DIM
    vnb = vn.astype(BF16)
    for c in range(ts // SGU_CHUNK):
        rows = slice(c * SGU_CHUNK, (c + 1) * SGU_CHUNK)
        stacked = _dot(sguw_ref[...], vnb[rows])
        mixed = stacked[(SGU_HEADS - 1) * SGU_CHUNK:]
        for hh in range(SGU_HEADS - 2, -1, -1):
            mixed = jnp.where(lane_head == hh, stacked[hh * SGU_CHUNK:(hh + 1) * SGU_CHUNK], mixed)
        y_ref[0, rows, 0:BRANCH_WIDTH] = (u[rows] * (mixed + sgub_ref[...])).astype(BF16)

    zb = _dot(h, w_ref[:, OFF_B:OFF_B + B_COLS])
    gate_b = zb[:, BRANCH_WIDTH:2 * BRANCH_WIDTH]
    yv = zb[:, 2 * BRANCH_WIDTH:] * zb[:, :BRANCH_WIDTH]
    ybuf_ref[CONV_HISTORY:CONV_HISTORY + ts, :] = yv
    conv = cb_ref[...] + yv * cw_ref[CONV_WIDTH - 1:CONV_WIDTH, :]
    for i in range(CONV_WIDTH - 1):
        back = CONV_WIDTH - 1 - i
        conv = conv + ybuf_ref[CONV_HISTORY - back:CONV_HISTORY - back + ts, :] * cw_ref[i:i + 1, :]
    ybuf_ref[0:CONV_HISTORY, :] = ybuf_ref[ts:ts + CONV_HISTORY, :]
    y_ref[0, :, BRANCH_WIDTH:2 * BRANCH_WIDTH] = (gate_b * conv).astype(BF16)

    zc = _dot(h, w_ref[:, OFF_C:OFF_C + C_COLS])
    ext = POOL_HISTORY + ts
    p0 = POOL_PAD
    zbuf_ref[0, p0 + POOL_HISTORY:p0 + ext, :] = zc
    s2 = zbuf_ref[0, p0:p0 + ext, :] + zbuf_ref[0, p0 - 1:p0 - 1 + ext, :]
    zbuf_ref[1, p0:p0 + ext, :] = s2
    s4 = s2 + zbuf_ref[1, p0 - 2:p0 - 2 + ext, :]
    zbuf_ref[2, p0:p0 + ext, :] = s4
    s8 = s4 + zbuf_ref[2, p0 - 4:p0 - 4 + ext, :]
    sums = {2: s2[POOL_HISTORY:], 4: s4[POOL_HISTORY:], 8: s8[POOL_HISTORY:],
            16: s8[POOL_HISTORY:] + s8[POOL_HISTORY - 8:POOL_HISTORY - 8 + ts]}
    zbuf_ref[0, p0:p0 + POOL_HISTORY, :] = zbuf_ref[0, p0 + ts:p0 + ext, :]
    lane_group = lane // POOL_GROUP_DIM
    wsum = sums[POOL_WINDOWS[-1]]
    win = jnp.full((1, BRANCH_WIDTH), POOL_WINDOWS[-1], jnp.int32)
    for gi in range(len(POOL_WINDOWS) - 2, -1, -1):
        wsum = jnp.where(lane_group == gi, sums[POOL_WINDOWS[gi]], wsum)
        win = jnp.where(lane_group == gi, POOL_WINDOWS[gi], win)
    tpos = si * ts + lax.broadcasted_iota(jnp.int32, (ts, 1), 0)
    count = jnp.minimum(tpos + 1, win).astype(F32)
    pooled = wsum / count - zc
    y_ref[0, :, 2 * BRANCH_WIDTH:3 * BRANCH_WIDTH] = (
        _dot(pooled.astype(BF16), pw_ref[...]) * ps_ref[...]).astype(BF16)

    zd = _dot(h, w_ref[:, OFF_D:OFF_D + QKVR_COLS + LANES])
    q = zd[:, 0:GLA_HK]
    k = zd[:, GLA_HK:2 * GLA_HK]
    v = zd[:, 2 * GLA_HK:2 * GLA_HK + BRANCH_WIDTH]
    r = zd[:, 2 * GLA_HK + BRANCH_WIDTH:QKVR_COLS]
    glr = zd[:, QKVR_COLS:]
    a = _dot(glr.astype(BF16), wg2_ref[...]) + bg_ref[...]
    log_g = (jnp.minimum(a, 0.0) - jnp.log(1.0 + jnp.exp(-jnp.abs(a)))) * (1.0 / GLA_TAU)
    blk = 4 * GLA_CHUNK
    rr = lax.broadcasted_iota(jnp.int32, (blk, blk), 0)
    cc = lax.broadcasted_iota(jnp.int32, (blk, blk), 1)
    same = (rr // GLA_CHUNK) == (cc // GLA_CHUNK)
    tri = jnp.where(same & (cc <= rr), 1.0, 0.0).astype(BF16)
    ones = jnp.where(same, 1.0, 0.0).astype(BF16)
    cums, lasts = [], []
    for b in range(ts // blk):
        hi, lo = _split(log_g[b * blk:(b + 1) * blk])
        cums.append(_dot(tri, hi) + _dot(tri, lo))
        lasts.append(_dot(ones, hi) + _dot(ones, lo))
    cum = jnp.concatenate(cums, axis=0)
    last = jnp.concatenate(lasts, axis=0)
    qd_all = (q * (GLA_DK ** -0.5) * jnp.exp(cum)).astype(BF16)
    ki_all = (k * jnp.exp(-cum)).astype(BF16)
    ke_all = (k * jnp.exp(last - cum)).astype(BF16)
    v_all = v.astype(BF16)
    decay_all = jnp.exp(last)

    nrow = GLA_HEADS * GLA_CHUNK
    row_head = lax.broadcasted_iota(jnp.int32, (nrow, 1), 0) // GLA_CHUNK
    mask_k = jnp.where(row_head == lax.broadcasted_iota(jnp.int32, (nrow, GLA_HK), 1) // GLA_DK,
                       1.0, 0.0).astype(BF16)
    mask_v = jnp.where(row_head == lax.broadcasted_iota(jnp.int32, (nrow, BRANCH_WIDTH), 1) // GLA_DV,
                       1.0, 0.0).astype(BF16)
    causal = (lax.broadcasted_iota(jnp.int32, (GLA_CHUNK, nrow), 1) % GLA_CHUNK
              <= lax.broadcasted_iota(jnp.int32, (GLA_CHUNK, nrow), 0))
    mask_s = (lax.broadcasted_iota(jnp.int32, (BRANCH_WIDTH, GLA_HK), 0) // GLA_DV
              == lax.broadcasted_iota(jnp.int32, (BRANCH_WIDTH, GLA_HK), 1) // GLA_DK)

    state = state_ref[...]
    outs = []
    for n in range(ts // GLA_CHUNK):
        rows = slice(n * GLA_CHUNK, (n + 1) * GLA_CHUNK)
        qd = qd_all[rows]
        vc = v_all[rows]
        kbd = jnp.concatenate([ki_all[rows]] * GLA_HEADS, axis=0) * mask_k
        vbd = jnp.concatenate([vc] * GLA_HEADS, axis=0) * mask_v
        scores = jnp.where(causal, _dot_nt(qd, kbd), 0.0)
        outs.append(_dot(scores.astype(BF16), vbd) + _dot_nt(qd, state.astype(BF16)))
        kv = _dot_tn(vc, ke_all[rows])
        state = state * decay_all[n * GLA_CHUNK:n * GLA_CHUNK + 1, :] + jnp.where(mask_s, kv, 0.0)
    state_ref[...] = state
    on = _group_rms(jnp.concatenate(outs, axis=0), bd64, gn_ref[...])
    y_ref[0, :, 3 * BRANCH_WIDTH:] = (_silu(r) * on).astype(BF16)


def _mixer_call(x, g, w_mix, sguw, sgub, sgun, cw, cb, pw, ps, wg2, bg, gn, cast_slabs=()):
    b, s, d = x.shape
    ts = MIX_TS
    n_si = s // ts
    steps = b * n_si

    def const(arr):
        return pl.BlockSpec(arr.shape, lambda bi, si: (0,) * arr.ndim)

    def slab(arr):
        rows = arr.shape[0]
        per = next(r for r in range(BF16_SUBLANES, rows + 1, BF16_SUBLANES)
                   if rows % r == 0 and r * steps >= rows)
        last = rows // per - 1
        return pl.BlockSpec((per, arr.shape[1]), lambda bi, si: (jnp.minimum(bi * n_si + si, last), 0))

    consts = (g, w_mix, sguw, sgub, sgun, cw, cb, pw, ps, wg2, bg, gn)
    return pl.pallas_call(
        functools.partial(_mixer_kernel, n_cast=len(cast_slabs)),
        grid=(b, n_si),
        in_specs=([pl.BlockSpec((1, ts, d), lambda bi, si: (bi, si, 0))] + [const(c) for c in consts]
                  + [slab(a) for a in cast_slabs]),
        out_specs=([pl.BlockSpec((1, ts, N_BRANCH * BRANCH_WIDTH), lambda bi, si: (bi, si, 0))]
                   + [slab(a) for a in cast_slabs]),
        out_shape=([jax.ShapeDtypeStruct((b, s, N_BRANCH * BRANCH_WIDTH), BF16)]
                   + [jax.ShapeDtypeStruct(a.shape, BF16) for a in cast_slabs]),
        scratch_shapes=[
            pltpu.VMEM((BRANCH_WIDTH, GLA_HK), F32),
            pltpu.VMEM((CONV_HISTORY + ts, BRANCH_WIDTH), F32),
            pltpu.VMEM((3, POOL_PAD + POOL_HISTORY + ts, BRANCH_WIDTH), F32),
        ],
        compiler_params=pltpu.CompilerParams(
            dimension_semantics=("arbitrary", "arbitrary"), vmem_limit_bytes=VMEM_LIMIT),
        name="mixer",
    )(x, *consts, *cast_slabs)


def _merge_kernel(x_ref, y_ref, g_ref, wg_ref, bgate_ref, p_ref, wo_ref, o_ref):
    x = x_ref[...]
    h = _rms(x, g_ref[...]).astype(BF16)
    merged = None
    for i in range(N_BRANCH):
        cols = slice(i * D_MODEL, (i + 1) * D_MODEL)
        gate = _sigmoid(_dot(h, wg_ref[:, cols]) + bgate_ref[:, cols])
        rows = slice(i * BRANCH_WIDTH, (i + 1) * BRANCH_WIDTH)
        term = gate * _dot(y_ref[:, rows], p_ref[rows, :])
        merged = term if merged is None else merged + term
    o_ref[...] = x + _dot(merged.astype(BF16), wo_ref[...])


def _row_spec(tm, width):
    return pl.BlockSpec((tm, width), lambda i: (i, 0))


def _const_spec(arr):
    return pl.BlockSpec(arr.shape, lambda i: (0,) * arr.ndim, pipeline_mode=pl.Buffered(1))


def _merge_call(x2, y2, g, wg, bgate, p, wo):
    t, d = x2.shape
    tm = DENSE_TILE
    consts = (g, wg, bgate, p, wo)
    return pl.pallas_call(
        _merge_kernel,
        grid=(t // tm,),
        in_specs=[_row_spec(tm, d), _row_spec(tm, y2.shape[1])] + [_const_spec(c) for c in consts],
        out_specs=_row_spec(tm, d),
        out_shape=jax.ShapeDtypeStruct((t, d), F32),
        compiler_params=pltpu.CompilerParams(
            dimension_semantics=("arbitrary",), vmem_limit_bytes=VMEM_LIMIT),
        name="merge",
    )(x2, y2, *consts)


FF_CHUNKS = D_FF // FF_CHUNK


def _swiglu_rows(hb, w1_ref, w3_ref, w2_ref, lead):
    acc = None
    for f in range(FF_CHUNKS):
        cols = slice(f * FF_CHUNK, (f + 1) * FF_CHUNK)
        up = _dot(hb, w1_ref[lead + (slice(None), cols)])
        gt = _dot(hb, w3_ref[lead + (slice(None), cols)])
        part = _dot((_silu(up) * gt).astype(BF16), w2_ref[lead + (cols, slice(None))])
        acc = part if acc is None else acc + part
    return acc


def _ffn_kernel(x_ref, g_ref, w1_ref, w3_ref, w2_ref, o_ref):
    x = x_ref[...]
    hb = _rms(x, g_ref[...]).astype(BF16)
    o_ref[...] = x + _swiglu_rows(hb, w1_ref, w3_ref, w2_ref, ())


def _ffn_call(x2, g, w1, w3, w2):
    t, d = x2.shape
    tm = DENSE_TILE
    consts = (g, w1, w3, w2)
    return pl.pallas_call(
        _ffn_kernel,
        grid=(t // tm,),
        in_specs=[_row_spec(tm, d)] + [_const_spec(c) for c in consts],
        out_specs=_row_spec(tm, d),
        out_shape=jax.ShapeDtypeStruct((t, d), F32),
        compiler_params=pltpu.CompilerParams(
            dimension_semantics=("arbitrary",), vmem_limit_bytes=VMEM_LIMIT),
        name="ffn",
    )(x2, *consts)


ROUTE_IDX0 = N_EXPERTS
ROUTE_W0 = N_EXPERTS + TOP_K
ROUTE_RANK0 = N_EXPERTS + 2 * TOP_K
CHOICE_LANES = LANES // TOP_K
SUBLANES = 8
D_TILES = D_MODEL // LANES


def _to_token_tiles(ref, value, lead=()):
    rows = value.shape[0]
    for c in range(D_TILES):
        ref[lead + (pl.ds(c, rows, stride=D_TILES), slice(None))] = value[:, c * LANES:(c + 1) * LANES]


def _from_token_tiles(ref, rows, lead=()):
    return jnp.concatenate(
        [ref[lead + (pl.ds(c, rows, stride=D_TILES), slice(None))] for c in range(D_TILES)], axis=1)


def _router_kernel(x_ref, g_ref, rhi_ref, rlo_ref, before_ref, h_ref, route_ref, route_t_ref, counts_ref, run_ref):
    step = pl.program_id(0)

    @pl.when(step == 0)
    def _():
        run_ref[...] = jnp.zeros_like(run_ref)

    x = x_ref[...]
    h = _rms(x, g_ref[...])
    _to_token_tiles(h_ref, h)
    hi, lo = _split(h)
    logits = _dot(hi, rhi_ref[...]) + _dot(hi, rlo_ref[...]) + _dot(lo, rhi_ref[...])
    lane = lax.broadcasted_iota(jnp.int32, logits.shape, 1)
    neg = jnp.float32(-jnp.inf)
    logits = jnp.where(lane < N_EXPERTS, logits, neg)
    m1 = jnp.max(logits, axis=-1, keepdims=True)
    i1 = jnp.min(jnp.where(logits == m1, lane, LANES), axis=-1, keepdims=True)
    rest = jnp.where(lane == i1, neg, logits)
    m2 = jnp.max(rest, axis=-1, keepdims=True)
    i2 = jnp.min(jnp.where(rest == m2, lane, LANES), axis=-1, keepdims=True)
    e2 = jnp.exp(m2 - m1)
    w1 = 1.0 / (1.0 + e2)
    w2 = e2 / (1.0 + e2)
    out = jnp.where(lane == ROUTE_IDX0, i1.astype(F32), 0.0)
    out = jnp.where(lane == ROUTE_IDX0 + 1, i2.astype(F32), out)
    out = jnp.where(lane == ROUTE_W0, w1, out)
    out = jnp.where(lane == ROUTE_W0 + 1, w2, out)
    onehots = [jnp.where(lane == idx + j * CHOICE_LANES, 1.0, 0.0) for j, idx in enumerate((i1, i2))]
    both = onehots[0] + onehots[1]
    seen = _dot(before_ref[...], both.astype(BF16)) + run_ref[0:1, :]
    for j in range(TOP_K):
        rank = jnp.sum(onehots[j] * seen, axis=-1, keepdims=True)
        out = jnp.where(lane == ROUTE_RANK0 + j, rank, out)
    run_ref[0:1, :] = run_ref[0:1, :] + jnp.sum(both, axis=0, keepdims=True)
    route_ref[...] = out
    route_t_ref[...] = out.T
    counts_ref[...] = run_ref[...]


def _router_call(x2, g, rhi, rlo):
    t, d = x2.shape
    tm = ROW_TILE
    before = jnp.tril(jnp.ones((tm, tm), BF16), -1)
    consts = (g, rhi, rlo, before)
    return pl.pallas_call(
        _router_kernel,
        grid=(t // tm,),
        in_specs=[_row_spec(tm, d)] + [_const_spec(c) for c in consts],
        out_specs=[_row_spec(tm * D_TILES, LANES), _row_spec(tm, LANES),
                   pl.BlockSpec((LANES, tm), lambda i: (0, i)), pl.BlockSpec((SUBLANES, LANES), lambda i: (0, 0))],
        out_shape=[jax.ShapeDtypeStruct((t * D_TILES, LANES), F32), jax.ShapeDtypeStruct((t, LANES), F32),
                   jax.ShapeDtypeStruct((LANES, t), F32), jax.ShapeDtypeStruct((SUBLANES, LANES), F32)],
        scratch_shapes=[pltpu.VMEM((SUBLANES, LANES), F32)],
        compiler_params=pltpu.CompilerParams(
            dimension_semantics=("arbitrary",), vmem_limit_bytes=VMEM_LIMIT),
        name="router",
    )(x2, *consts)


GATHER_UNROLL = 16
DMA_PRIORITIES = 2


def _token_tile(ref, r):
    return ref.at[pl.ds(pl.multiple_of(r * D_TILES, D_TILES), D_TILES), :]


def _start_row_gather(idx_ref, src_hbm, dst, sem, priorities=1):
    def body(g, c):
        for u in range(GATHER_UNROLL):
            r = g * GATHER_UNROLL + u
            pltpu.make_async_copy(_token_tile(src_hbm, idx_ref[0, 0, r]), _token_tile(dst, r), sem).start(
                priority=u % priorities)
        return c

    lax.fori_loop(0, dst.shape[0] // D_TILES // GATHER_UNROLL, body, 0)


def _wait_row_gather(src_hbm, dst, sem):
    pltpu.make_async_copy(src_hbm.at[pl.ds(0, dst.shape[0]), :], dst, sem).wait()


def _group_kernel(texp_ref, tvalid_ref, rows_ref, rows_next_ref, h_hbm, w1_ref, w3_ref, w2_ref, ys_ref,
                  xbuf, sem):
    i = pl.program_id(0)
    slot = i % 2

    @pl.when(i == 0)
    def _():
        _start_row_gather(rows_ref, h_hbm, xbuf.at[0], sem.at[0], DMA_PRIORITIES)

    @pl.when(i + 1 < pl.num_programs(0))
    def _():
        _start_row_gather(rows_next_ref, h_hbm, xbuf.at[1 - slot], sem.at[1 - slot], DMA_PRIORITIES)

    _wait_row_gather(h_hbm, xbuf.at[slot], sem.at[slot])

    @pl.when(tvalid_ref[i] == 1)
    def _():
        xb = _from_token_tiles(xbuf, GROUP_TILE, (slot,)).astype(BF16)
        _to_token_tiles(ys_ref, _swiglu_rows(xb, w1_ref, w3_ref, w2_ref, (0,)))

    @pl.when(tvalid_ref[i] == 0)
    def _():
        ys_ref[...] = jnp.zeros_like(ys_ref)


def _group_call(tile_expert, tile_valid, row_token, h3, w1, w3, w2):
    n_tiles = tile_expert.shape[0]
    tm = GROUP_TILE
    rows = row_token.reshape(n_tiles, 1, tm)

    def wspec(arr):
        return pl.BlockSpec((1,) + arr.shape[1:], lambda i, te, tv: (te[i], 0, 0))

    grid_spec = pltpu.PrefetchScalarGridSpec(
        num_scalar_prefetch=2,
        grid=(n_tiles,),
        in_specs=[
            pl.BlockSpec((1, 1, tm), lambda i, te, tv: (i, 0, 0), memory_space=pltpu.SMEM),
            pl.BlockSpec((1, 1, tm), lambda i, te, tv: (jnp.minimum(i + 1, n_tiles - 1), 0, 0),
                         memory_space=pltpu.SMEM),
            pl.BlockSpec(memory_space=pl.ANY),
            wspec(w1), wspec(w3), wspec(w2),
        ],
        out_specs=pl.BlockSpec((tm * D_TILES, LANES), lambda i, te, tv: (i, 0)),
        scratch_shapes=[pltpu.VMEM((2, tm * D_TILES, LANES), F32), pltpu.SemaphoreType.DMA((2,))],
    )
    return pl.pallas_call(
        _group_kernel,
        grid_spec=grid_spec,
        out_shape=jax.ShapeDtypeStruct((n_tiles * tm * D_TILES, LANES), F32),
        compiler_params=pltpu.CompilerParams(
            dimension_semantics=("arbitrary",), vmem_limit_bytes=VMEM_LIMIT),
        name="moe_group",
    )(tile_expert, tile_valid, rows, rows, h3, w1, w3, w2)


def _combine_kernel(p0_ref, p1_ref, p0_next_ref, p1_next_ref, ys_hbm, x_ref, route_ref, g_ref, o_ref,
                    buf, sem, *, apply_final_norm):
    i = pl.program_id(0)
    slot = i % 2

    def start(refs, s):
        for j in range(TOP_K):
            _start_row_gather(refs[j], ys_hbm, buf.at[s, j], sem.at[s, j], DMA_PRIORITIES)

    @pl.when(i == 0)
    def _():
        start((p0_ref, p1_ref), 0)

    @pl.when(i + 1 < pl.num_programs(0))
    def _():
        start((p0_next_ref, p1_next_ref), 1 - slot)

    for j in range(TOP_K):
        _wait_row_gather(ys_hbm, buf.at[slot, j], sem.at[slot, j])
    route = route_ref[...]
    y = x_ref[...]
    for j in range(TOP_K):
        y = y + route[:, ROUTE_W0 + j:ROUTE_W0 + j + 1] * _from_token_tiles(buf, ROW_TILE, (slot, j))
    o_ref[...] = _rms(y, g_ref[...]) if apply_final_norm else y


def _combine_call(pos0, pos1, ys, x2, route, g, apply_final_norm):
    t, d = x2.shape
    tm = ROW_TILE
    n = t // tm

    def ispec(shift):
        return pl.BlockSpec((1, 1, tm), lambda i: (jnp.minimum(i + shift, n - 1), 0, 0),
                            memory_space=pltpu.SMEM)

    p0 = pos0.reshape(n, 1, tm)
    p1 = pos1.reshape(n, 1, tm)
    return pl.pallas_call(
        functools.partial(_combine_kernel, apply_final_norm=apply_final_norm),
        grid=(n,),
        in_specs=[ispec(0), ispec(0), ispec(1), ispec(1), pl.BlockSpec(memory_space=pl.ANY),
                  _row_spec(tm, d), _row_spec(tm, LANES), _const_spec(g)],
        out_specs=_row_spec(tm, d),
        out_shape=jax.ShapeDtypeStruct((t, d), F32),
        scratch_shapes=[pltpu.VMEM((2, TOP_K, tm * D_TILES, LANES), F32),
                        pltpu.SemaphoreType.DMA((2, TOP_K))],
        compiler_params=pltpu.CompilerParams(
            dimension_semantics=("arbitrary",), vmem_limit_bytes=VMEM_LIMIT),
        name="moe_combine",
    )(p0, p1, p0, p1, ys, x2, route, g)


def _final_norm_kernel(x_ref, g_ref, o_ref):
    o_ref[...] = _rms(x_ref[...], g_ref[...])


def _final_norm_call(x2, g):
    t, d = x2.shape
    tm = ROW_TILE
    return pl.pallas_call(
        _final_norm_kernel,
        grid=(t // tm,),
        in_specs=[_row_spec(tm, d), _const_spec(g)],
        out_specs=_row_spec(tm, d),
        out_shape=jax.ShapeDtypeStruct((t, d), F32),
        name="final_norm",
    )(x2, g)


def _routing_plan(route_t, choice_counts, n_tiles):
    t = route_t.shape[1]
    tm = GROUP_TILE
    choice = [route_t[ROUTE_IDX0 + j].astype(jnp.int32) for j in range(TOP_K)]
    rank = [route_t[ROUTE_RANK0 + j].astype(jnp.int32) for j in range(TOP_K)]
    per_choice = jnp.stack([choice_counts[0, j * CHOICE_LANES:j * CHOICE_LANES + N_EXPERTS]
                            for j in range(TOP_K)]).astype(jnp.int32)
    counts = jnp.sum(per_choice, axis=0)
    tiles_per = (counts + tm - 1) // tm
    tile_end = jnp.cumsum(tiles_per)
    tile_start = tile_end - tiles_per
    earlier = jnp.cumsum(per_choice, axis=0) - per_choice
    pos = [tile_start[choice[j]] * tm + earlier[j][choice[j]] + rank[j] for j in range(TOP_K)]
    experts = jnp.concatenate(choice)
    tile_ids = jnp.arange(n_tiles, dtype=jnp.int32)
    tile_expert = jnp.minimum(
        jnp.sum((tile_ids[:, None] >= tile_end[None, :]).astype(jnp.int32), axis=1), N_EXPERTS - 1)
    tile_valid = (tile_ids < tile_end[-1]).astype(jnp.int32)
    order = jnp.argsort(experts, stable=True).astype(jnp.int32)
    row_expert = jnp.repeat(tile_expert, tm)
    row_rank = jnp.arange(n_tiles * tm, dtype=jnp.int32) - tile_start[row_expert] * tm
    row_used = (row_rank < counts[row_expert]) & (jnp.repeat(tile_valid, tm) == 1)
    src = jnp.clip((jnp.cumsum(counts) - counts)[row_expert] + row_rank, 0, t * TOP_K - 1)
    row_token = jnp.where(row_used, order[src] % t, 0)
    last_expert = tile_expert[jnp.maximum(tile_end[-1] - 1, 0)]
    tile_expert = jnp.where(tile_valid == 1, tile_expert, last_expert)
    return tile_expert, tile_valid, row_token, pos[0], pos[1]


def _pack_mixer_weights(w_in):
    glr = jnp.pad(w_in[:, OFF_GLR:GATE_OFF], ((0, 0), (0, LANES - GLA_RANK)))
    return jnp.concatenate([w_in[:, :OFF_GLR], glr], axis=1).astype(BF16)


def _row(v):
    return v.reshape(1, -1).astype(F32)


def kernel(x, norm_mix, w_in, b_gate, sgu_w, sgu_b, sgu_norm, conv_w, conv_b, pool_w, pool_scale, gla_wg2, gla_bg, gla_norm, branch_proj, w_out, norm_ffn, ffn_w1, ffn_w3, ffn_w2, moe_router, moe_w1, moe_w3, moe_w2, final_norm):
    b, s, d = x.shape
    t = b * s
    depth = norm_mix.shape[0]
    causal = jnp.tril(jnp.ones((SGU_CHUNK, SGU_CHUNK), dtype=bool))
    out = None
    for l in range(depth):
        w_mix = _pack_mixer_weights(w_in[l])
        sguw = jnp.concatenate([jnp.where(causal, sgu_w[l, hh], 0.0) for hh in range(SGU_HEADS)],
                               axis=0).astype(BF16)
        sgub = jnp.repeat(sgu_b[l].T, SGU_HEAD_DIM, axis=1).astype(F32)
        pw = jax.scipy.linalg.block_diag(*[pool_w[l, gi] for gi in range(len(POOL_WINDOWS))]).astype(BF16)
        wg2 = jnp.pad(gla_wg2[l], ((0, LANES - GLA_RANK), (0, 0))).astype(BF16)
        gn = jnp.tile(gla_norm[l], GLA_HEADS)
        i = l // 2
        slabs = {}
        if l == 0:
            slabs["branch_proj"] = branch_proj.reshape(-1, d)
            slabs["w_out"] = w_out.reshape(-1, d)
        if l % 2 == 0:
            slabs["ffn_w1"] = ffn_w1[i]
            slabs["ffn_w3"] = ffn_w3[i]
            slabs["ffn_w2"] = ffn_w2[i]
            if l + 1 < depth:
                slabs["moe_w1"] = moe_w1[(l + 1) // 2].reshape(-1, D_FF)
                slabs["moe_w3"] = moe_w3[(l + 1) // 2].reshape(-1, D_FF)
        else:
            slabs["moe_w2"] = moe_w2[i].reshape(-1, d)
        y, *cast = _mixer_call(x, _row(norm_mix[l]), w_mix, sguw, sgub, _row(sgu_norm[l]),
                               conv_w[l].astype(F32), _row(conv_b[l]), pw, _row(pool_scale[l]),
                               wg2, _row(gla_bg[l]), _row(gn), tuple(slabs.values()))
        cast = dict(zip(slabs, cast))
        if l == 0:
            proj_b = cast["branch_proj"].reshape(depth, N_BRANCH * BRANCH_WIDTH, d)
            w_out_b = cast["w_out"].reshape(depth, d, d)
        if "moe_w1" in cast:
            moe_up = [cast[k].reshape(N_EXPERTS, d, D_FF) for k in ("moe_w1", "moe_w3")]
        if "moe_w2" in cast:
            moe_down = cast["moe_w2"].reshape(N_EXPERTS, D_FF, d)
        x2 = _merge_call(x.reshape(t, d), y.reshape(t, N_BRANCH * BRANCH_WIDTH), _row(norm_mix[l]),
                         w_in[l][:, GATE_OFF:].astype(BF16), _row(b_gate[l]), proj_b[l], w_out_b[l])
        if l % 2 == 0:
            x2 = _ffn_call(x2, _row(norm_ffn[l]), cast["ffn_w1"], cast["ffn_w3"], cast["ffn_w2"])
            if l == depth - 1:
                out = _final_norm_call(x2, _row(final_norm))
        else:
            rw = jnp.pad(moe_router[i], ((0, 0), (0, LANES - N_EXPERTS)))
            rhi = rw.astype(BF16)
            h2, route, route_t, choice_counts = _router_call(
                x2, _row(norm_ffn[l]), rhi, (rw - rhi.astype(F32)).astype(BF16))
            n_tiles = (t * TOP_K) // GROUP_TILE + N_EXPERTS
            tile_expert, tile_valid, row_token, pos0, pos1 = _routing_plan(route_t, choice_counts, n_tiles)
            ys = _group_call(tile_expert, tile_valid, row_token, h2, moe_up[0], moe_up[1], moe_down)
            x2 = _combine_call(pos0, pos1, ys, x2, route, _row(final_norm), l == depth - 1)
            if l == depth - 1:
                out = x2
        x = x2.reshape(b, s, d)
    return out.reshape(b, s, d)
```

```python
import functools

import numpy as np
import jax
import jax.numpy as jnp
from jax import lax
from jax.experimental import pallas as pl
from jax.experimental.pallas import tpu as pltpu

F32 = jnp.float32
BF16 = jnp.bfloat16

D_MODEL = 1024
N_BRANCH = 4
BRANCH_WIDTH = 256
SGU_HEADS = 4
SGU_HEAD_DIM = BRANCH_WIDTH // SGU_HEADS
SGU_CHUNK = 128
CONV_WIDTH = 3
POOL_WINDOWS = (2, 4, 8, 16)
POOL_GROUP_DIM = BRANCH_WIDTH // len(POOL_WINDOWS)
POOL_HISTORY = 16
POOL_PAD = 8
assert POOL_WINDOWS == (2, 4, 8, 16)
CONV_HISTORY = 8
GLA_HEADS = 4
GLA_DK = 32
GLA_DV = BRANCH_WIDTH // GLA_HEADS
GLA_HK = GLA_HEADS * GLA_DK
GLA_RANK = 16
GLA_TAU = 16.0
GLA_CHUNK = 64
D_FF = 2816
N_EXPERTS = 8
TOP_K = 2
EPS = 1e-6

A_COLS = 2 * BRANCH_WIDTH
B_COLS = 3 * BRANCH_WIDTH
C_COLS = BRANCH_WIDTH
QKVR_COLS = 2 * GLA_HK + 2 * BRANCH_WIDTH
LANES = 128
BF16_SUBLANES = 16
OFF_A = 0
OFF_B = OFF_A + A_COLS
OFF_C = OFF_B + B_COLS
OFF_D = OFF_C + C_COLS
OFF_GLR = OFF_D + QKVR_COLS
MIX_COLS = OFF_GLR + LANES
GATE_OFF = OFF_GLR + GLA_RANK

MIX_TS = 512
ROW_TILE = 512
DENSE_TILE = 1024
GROUP_TILE = 512
FF_CHUNK = 256
VMEM_LIMIT = 56 * 1024 * 1024


def _dot(a, b):
    return jnp.dot(a, b, preferred_element_type=F32)


def _dot_nt(a, b):
    return lax.dot_general(a, b, (((1,), (1,)), ((), ())), preferred_element_type=F32)


def _dot_tn(a, b):
    return lax.dot_general(a, b, (((0,), (0,)), ((), ())), preferred_element_type=F32)


def _split(x):
    hi = x.astype(BF16)
    lo = (x - hi.astype(F32)).astype(BF16)
    return hi, lo


def _rms(x, g):
    ms = jnp.mean(x * x, axis=-1, keepdims=True)
    return x * lax.rsqrt(ms + EPS) * g


def _sigmoid(x):
    return 1.0 / (1.0 + jnp.exp(-x))


def _silu(x):
    return x * _sigmoid(x)


def _gelu_tanh(x):
    c = np.float32(np.sqrt(2.0 / np.pi))
    return x * (0.5 * (1.0 + jnp.tanh(c * (x + 0.044715 * (x * x * x)))))


def _group_mean_matrix(n, group):
    r = lax.broadcasted_iota(jnp.int32, (n, n), 0) // group
    c = lax.broadcasted_iota(jnp.int32, (n, n), 1) // group
    return jnp.where(r == c, 1.0 / group, 0.0).astype(BF16)


def _group_rms(x, bd, g):
    ms = _dot((x * x).astype(BF16), bd)
    return x * lax.rsqrt(ms + EPS) * g


def _mixer_kernel(x_ref, g_ref, w_ref, sguw_ref, sgub_ref, sgun_ref, cw_ref, cb_ref, pw_ref, ps_ref,
                  wg2_ref, bg_ref, gn_ref, *rest, n_cast):
    cast_in = rest[:n_cast]
    y_ref = rest[n_cast]
    cast_out = rest[n_cast + 1:2 * n_cast + 1]
    state_ref, ybuf_ref, zbuf_ref = rest[2 * n_cast + 1:]
    for src, dst in zip(cast_in, cast_out):
        dst[...] = src[...].astype(BF16)
    si = pl.program_id(1)
    ts = x_ref.shape[1]
    x = x_ref[0]
    h = _rms(x, g_ref[...]).astype(BF16)

    @pl.when(si == 0)
    def _():
        state_ref[...] = jnp.zeros_like(state_ref)
        ybuf_ref[0:CONV_HISTORY, :] = jnp.zeros((CONV_HISTORY, BRANCH_WIDTH), F32)
        zbuf_ref[:, 0:POOL_PAD + POOL_HISTORY, :] = jnp.zeros((3, POOL_PAD + POOL_HISTORY, BRANCH_WIDTH), F32)

    bd64 = _group_mean_matrix(BRANCH_WIDTH, SGU_HEAD_DIM)
    lane = lax.broadcasted_iota(jnp.int32, (1, BRANCH_WIDTH), 1)

    z = _gelu_tanh(_dot(h, w_ref[:, OFF_A:OFF_A + A_COLS]))
    u = z[:, :BRANCH_WIDTH]
    vn = _group_rms(z[:, BRANCH_WIDTH:], bd64, sgun_ref[...])
    lane_head = lane // SGU_HEAD_DIM
    vnb = vn.astype(BF16)
    for c in range(ts // SGU_CHUNK):
        rows = slice(c * SGU_CHUNK, (c + 1) * SGU_CHUNK)
        stacked = _dot(sguw_ref[...], vnb[rows])
        mixed = stacked[(SGU_HEADS - 1) * SGU_CHUNK:]
        for hh in range(SGU_HEADS - 2, -1, -1):
            mixed = jnp.where(lane_head == hh, stacked[hh * SGU_CHUNK:(hh + 1) * SGU_CHUNK], mixed)
        y_ref[0, rows, 0:BRANCH_WIDTH] = (u[rows] * (mixed + sgub_ref[...])).astype(BF16)

    zb = _dot(h, w_ref[:, OFF_B:OFF_B + B_COLS])
    gate_b = zb[:, BRANCH_WIDTH:2 * BRANCH_WIDTH]
    yv = zb[:, 2 * BRANCH_WIDTH:] * zb[:, :BRANCH_WIDTH]
    ybuf_ref[CONV_HISTORY:CONV_HISTORY + ts, :] = yv
    conv = cb_ref[...] + yv * cw_ref[CONV_WIDTH - 1:CONV_WIDTH, :]
    for i in range(CONV_WIDTH - 1):
        back = CONV_WIDTH - 1 - i
        conv = conv + ybuf_ref[CONV_HISTORY - back:CONV_HISTORY - back + ts, :] * cw_ref[i:i + 1, :]
    ybuf_ref[0:CONV_HISTORY, :] = ybuf_ref[ts:ts + CONV_HISTORY, :]
    y_ref[0, :, BRANCH_WIDTH:2 * BRANCH_WIDTH] = (gate_b * conv).astype(BF16)

    zc = _dot(h, w_ref[:, OFF_C:OFF_C + C_COLS])
    ext = POOL_HISTORY + ts
    p0 = POOL_PAD
    zbuf_ref[0, p0 + POOL_HISTORY:p0 + ext, :] = zc
    s2 = zbuf_ref[0, p0:p0 + ext, :] + zbuf_ref[0, p0 - 1:p0 - 1 + ext, :]
    zbuf_ref[1, p0:p0 + ext, :] = s2
    s4 = s2 + zbuf_ref[1, p0 - 2:p0 - 2 + ext, :]
    zbuf_ref[2, p0:p0 + ext, :] = s4
    s8 = s4 + zbuf_ref[2, p0 - 4:p0 - 4 + ext, :]
    sums = {2: s2[POOL_HISTORY:], 4: s4[POOL_HISTORY:], 8: s8[POOL_HISTORY:],
            16: s8[POOL_HISTORY:] + s8[POOL_HISTORY - 8:POOL_HISTORY - 8 + ts]}
    zbuf_ref[0, p0:p0 + POOL_HISTORY, :] = zbuf_ref[0, p0 + ts:p0 + ext, :]
    lane_group = lane // POOL_GROUP_DIM
    wsum = sums[POOL_WINDOWS[-1]]
    win = jnp.full((1, BRANCH_WIDTH), POOL_WINDOWS[-1], jnp.int32)
    for gi in range(len(POOL_WINDOWS) - 2, -1, -1):
        wsum = jnp.where(lane_group == gi, sums[POOL_WINDOWS[gi]], wsum)
        win = jnp.where(lane_group == gi, POOL_WINDOWS[gi], win)
    tpos = si * ts + lax.broadcasted_iota(jnp.int32, (ts, 1), 0)
    count = jnp.minimum(tpos + 1, win).astype(F32)
    pooled = wsum / count - zc
    y_ref[0, :, 2 * BRANCH_WIDTH:3 * BRANCH_WIDTH] = (
        _dot(pooled.astype(BF16), pw_ref[...]) * ps_ref[...]).astype(BF16)

    zd = _dot(h, w_ref[:, OFF_D:OFF_D + QKVR_COLS + LANES])
    q = zd[:, 0:GLA_HK]
    k = zd[:, GLA_HK:2 * GLA_HK]
    v = zd[:, 2 * GLA_HK:2 * GLA_HK + BRANCH_WIDTH]
    r = zd[:, 2 * GLA_HK + BRANCH_WIDTH:QKVR_COLS]
    glr = zd[:, QKVR_COLS:]
    a = _dot(glr.astype(BF16), wg2_ref[...]) + bg_ref[...]
    log_g = (jnp.minimum(a, 0.0) - jnp.log(1.0 + jnp.exp(-jnp.abs(a)))) * (1.0 / GLA_TAU)
    blk = 4 * GLA_CHUNK
    rr = lax.broadcasted_iota(jnp.int32, (blk, blk), 0)
    cc = lax.broadcasted_iota(jnp.int32, (blk, blk), 1)
    same = (rr // GLA_CHUNK) == (cc // GLA_CHUNK)
    tri = jnp.where(same & (cc <= rr), 1.0, 0.0).astype(BF16)
    cums = []
    for b in range(ts // blk):
        hi, lo = _split(log_g[b * blk:(b + 1) * blk])
        cums.append(_dot(tri, hi) + _dot(tri, lo))
    cum = jnp.concatenate(cums, axis=0)
    by_chunk = cum.reshape(ts // GLA_CHUNK, GLA_CHUNK, GLA_HK)
    last = jnp.broadcast_to(by_chunk[:, GLA_CHUNK - 1:, :], by_chunk.shape).reshape(ts, GLA_HK)
    qd_all = (q * (GLA_DK ** -0.5) * jnp.exp(cum)).astype(BF16)
    ki_all = (k * jnp.exp(-cum)).astype(BF16)
    ke_all = (k * jnp.exp(last - cum)).astype(BF16)
    v_all = v.astype(BF16)
    decay_all = jnp.exp(last)

    nrow = GLA_HEADS * GLA_CHUNK
    row_head = lax.broadcasted_iota(jnp.int32, (nrow, 1), 0) // GLA_CHUNK
    mask_k = jnp.where(row_head == lax.broadcasted_iota(jnp.int32, (nrow, GLA_HK), 1) // GLA_DK,
                       1.0, 0.0).astype(BF16)
    mask_v = jnp.where(row_head == lax.broadcasted_iota(jnp.int32, (nrow, BRANCH_WIDTH), 1) // GLA_DV,
                       1.0, 0.0).astype(BF16)
    causal = (lax.broadcasted_iota(jnp.int32, (GLA_CHUNK, nrow), 1) % GLA_CHUNK
              <= lax.broadcasted_iota(jnp.int32, (GLA_CHUNK, nrow), 0))
    mask_s = (lax.broadcasted_iota(jnp.int32, (BRANCH_WIDTH, GLA_HK), 0) // GLA_DV
              == lax.broadcasted_iota(jnp.int32, (BRANCH_WIDTH, GLA_HK), 1) // GLA_DK)

    state = state_ref[...]
    outs = []
    for n in range(ts // GLA_CHUNK):
        rows = slice(n * GLA_CHUNK, (n + 1) * GLA_CHUNK)
        qd = qd_all[rows]
        vc = v_all[rows]
        kbd = jnp.concatenate([ki_all[rows]] * GLA_HEADS, axis=0) * mask_k
        vbd = jnp.concatenate([vc] * GLA_HEADS, axis=0) * mask_v
        scores = jnp.where(causal, _dot_nt(qd, kbd), 0.0)
        outs.append(_dot(scores.astype(BF16), vbd) + _dot_nt(qd, state.astype(BF16)))
        kv = _dot_tn(vc, ke_all[rows])
        state = state * decay_all[n * GLA_CHUNK:n * GLA_CHUNK + 1, :] + jnp.where(mask_s, kv, 0.0)
    state_ref[...] = state
    on = _group_rms(jnp.concatenate(outs, axis=0), bd64, gn_ref[...])
    y_ref[0, :, 3 * BRANCH_WIDTH:] = (_silu(r) * on).astype(BF16)


def _mixer_call(x, g, w_mix, sguw, sgub, sgun, cw, cb, pw, ps, wg2, bg, gn, cast_slabs=()):
    b, s, d = x.shape
    ts = MIX_TS
    n_si = s // ts
    steps = b * n_si

    def const(arr):
        return pl.BlockSpec(arr.shape, lambda bi, si: (0,) * arr.ndim)

    def slab(arr):
        rows = arr.shape[0]
        per = next(r for r in range(BF16_SUBLANES, rows + 1, BF16_SUBLANES)
                   if rows % r == 0 and r * steps >= rows)
        last = rows // per - 1
        return pl.BlockSpec((per, arr.shape[1]), lambda bi, si: (jnp.minimum(bi * n_si + si, last), 0))

    consts = (g, w_mix, sguw, sgub, sgun, cw, cb, pw, ps, wg2, bg, gn)
    return pl.pallas_call(
        functools.partial(_mixer_kernel, n_cast=len(cast_slabs)),
        grid=(b, n_si),
        in_specs=([pl.BlockSpec((1, ts, d), lambda bi, si: (bi, si, 0))] + [const(c) for c in consts]
                  + [slab(a) for a in cast_slabs]),
        out_specs=([pl.BlockSpec((1, ts, N_BRANCH * BRANCH_WIDTH), lambda bi, si: (bi, si, 0))]
                   + [slab(a) for a in cast_slabs]),
        out_shape=([jax.ShapeDtypeStruct((b, s, N_BRANCH * BRANCH_WIDTH), BF16)]
                   + [jax.ShapeDtypeStruct(a.shape, BF16) for a in cast_slabs]),
        scratch_shapes=[
            pltpu.VMEM((BRANCH_WIDTH, GLA_HK), F32),
            pltpu.VMEM((CONV_HISTORY + ts, BRANCH_WIDTH), F32),
            pltpu.VMEM((3, POOL_PAD + POOL_HISTORY + ts, BRANCH_WIDTH), F32),
        ],
        compiler_params=pltpu.CompilerParams(
            dimension_semantics=("arbitrary", "arbitrary"), vmem_limit_bytes=VMEM_LIMIT),
        name="mixer",
    )(x, *consts, *cast_slabs)


def _merge_kernel(x_ref, y_ref, g_ref, wg_ref, bgate_ref, p_ref, wo_ref, o_ref):
    x = x_ref[...]
    h = _rms(x, g_ref[...]).astype(BF16)
    merged = None
    for i in range(N_BRANCH):
        cols = slice(i * D_MODEL, (i + 1) * D_MODEL)
        gate = _sigmoid(_dot(h, wg_ref[:, cols]) + bgate_ref[:, cols])
        rows = slice(i * BRANCH_WIDTH, (i + 1) * BRANCH_WIDTH)
        term = gate * _dot(y_ref[:, rows], p_ref[rows, :])
        merged = term if merged is None else merged + term
    o_ref[...] = x + _dot(merged.astype(BF16), wo_ref[...])


def _row_spec(tm, width):
    return pl.BlockSpec((tm, width), lambda i: (i, 0))


def _const_spec(arr):
    return pl.BlockSpec(arr.shape, lambda i: (0,) * arr.ndim, pipeline_mode=pl.Buffered(1))


def _merge_call(x2, y2, g, wg, bgate, p, wo):
    t, d = x2.shape
    tm = DENSE_TILE
    consts = (g, wg, bgate, p, wo)
    return pl.pallas_call(
        _merge_kernel,
        grid=(t // tm,),
        in_specs=[_row_spec(tm, d), _row_spec(tm, y2.shape[1])] + [_const_spec(c) for c in consts],
        out_specs=_row_spec(tm, d),
        out_shape=jax.ShapeDtypeStruct((t, d), F32),
        compiler_params=pltpu.CompilerParams(
            dimension_semantics=("arbitrary",), vmem_limit_bytes=VMEM_LIMIT),
        name="merge",
    )(x2, y2, *consts)


FF_CHUNKS = D_FF // FF_CHUNK


def _swiglu_rows(hb, w1_ref, w3_ref, w2_ref, lead):
    acc = None
    for f in range(FF_CHUNKS):
        cols = slice(f * FF_CHUNK, (f + 1) * FF_CHUNK)
        up = _dot(hb, w1_ref[lead + (slice(None), cols)])
        gt = _dot(hb, w3_ref[lead + (slice(None), cols)])
        part = _dot((_silu(up) * gt).astype(BF16), w2_ref[lead + (cols, slice(None))])
        acc = part if acc is None else acc + part
    return acc


def _ffn_kernel(x_ref, g_ref, w1_ref, w3_ref, w2_ref, o_ref):
    x = x_ref[...]
    hb = _rms(x, g_ref[...]).astype(BF16)
    o_ref[...] = x + _swiglu_rows(hb, w1_ref, w3_ref, w2_ref, ())


def _ffn_call(x2, g, w1, w3, w2):
    t, d = x2.shape
    tm = DENSE_TILE
    consts = (g, w1, w3, w2)
    return pl.pallas_call(
        _ffn_kernel,
        grid=(t // tm,),
        in_specs=[_row_spec(tm, d)] + [_const_spec(c) for c in consts],
        out_specs=_row_spec(tm, d),
        out_shape=jax.ShapeDtypeStruct((t, d), F32),
        compiler_params=pltpu.CompilerParams(
            dimension_semantics=("arbitrary",), vmem_limit_bytes=VMEM_LIMIT),
        name="ffn",
    )(x2, *consts)


ROUTE_IDX0 = N_EXPERTS
ROUTE_W0 = N_EXPERTS + TOP_K
ROUTE_RANK0 = N_EXPERTS + 2 * TOP_K
CHOICE_LANES = LANES // TOP_K
SUBLANES = 8
D_TILES = D_MODEL // LANES


def _to_token_tiles(ref, value, lead=()):
    rows = value.shape[0]
    for c in range(D_TILES):
        ref[lead + (pl.ds(c, rows, stride=D_TILES), slice(None))] = value[:, c * LANES:(c + 1) * LANES]


def _from_token_tiles(ref, rows, lead=()):
    return jnp.concatenate(
        [ref[lead + (pl.ds(c, rows, stride=D_TILES), slice(None))] for c in range(D_TILES)], axis=1)


def _router_kernel(x_ref, g_ref, rhi_ref, rlo_ref, before_ref, h_ref, route_ref, route_t_ref, counts_ref, run_ref):
    step = pl.program_id(0)

    @pl.when(step == 0)
    def _():
        run_ref[...] = jnp.zeros_like(run_ref)

    x = x_ref[...]
    h = _rms(x, g_ref[...])
    _to_token_tiles(h_ref, h)
    hi, lo = _split(h)
    logits = _dot(hi, rhi_ref[...]) + _dot(hi, rlo_ref[...]) + _dot(lo, rhi_ref[...])
    lane = lax.broadcasted_iota(jnp.int32, logits.shape, 1)
    neg = jnp.float32(-jnp.inf)
    logits = jnp.where(lane < N_EXPERTS, logits, neg)
    m1 = jnp.max(logits, axis=-1, keepdims=True)
    i1 = jnp.min(jnp.where(logits == m1, lane, LANES), axis=-1, keepdims=True)
    rest = jnp.where(lane == i1, neg, logits)
    m2 = jnp.max(rest, axis=-1, keepdims=True)
    i2 = jnp.min(jnp.where(rest == m2, lane, LANES), axis=-1, keepdims=True)
    e2 = jnp.exp(m2 - m1)
    w1 = 1.0 / (1.0 + e2)
    w2 = e2 / (1.0 + e2)
    out = jnp.where(lane == ROUTE_IDX0, i1.astype(F32), 0.0)
    out = jnp.where(lane == ROUTE_IDX0 + 1, i2.astype(F32), out)
    out = jnp.where(lane == ROUTE_W0, w1, out)
    out = jnp.where(lane == ROUTE_W0 + 1, w2, out)
    onehots = [jnp.where(lane == idx + j * CHOICE_LANES, 1.0, 0.0) for j, idx in enumerate((i1, i2))]
    both = onehots[0] + onehots[1]
    seen = _dot(before_ref[...], both.astype(BF16)) + run_ref[0:1, :]
    for j in range(TOP_K):
        rank = jnp.sum(onehots[j] * seen, axis=-1, keepdims=True)
        out = jnp.where(lane == ROUTE_RANK0 + j, rank, out)
    run_ref[0:1, :] = run_ref[0:1, :] + jnp.sum(both, axis=0, keepdims=True)
    route_ref[...] = out
    route_t_ref[...] = out.T
    counts_ref[...] = run_ref[...]


def _router_call(x2, g, rhi, rlo):
    t, d = x2.shape
    tm = ROW_TILE
    before = jnp.tril(jnp.ones((tm, tm), BF16), -1)
    consts = (g, rhi, rlo, before)
    return pl.pallas_call(
        _router_kernel,
        grid=(t // tm,),
        in_specs=[_row_spec(tm, d)] + [_const_spec(c) for c in consts],
        out_specs=[_row_spec(tm * D_TILES, LANES), _row_spec(tm, LANES),
                   pl.BlockSpec((LANES, tm), lambda i: (0, i)), pl.BlockSpec((SUBLANES, LANES), lambda i: (0, 0))],
        out_shape=[jax.ShapeDtypeStruct((t * D_TILES, LANES), F32), jax.ShapeDtypeStruct((t, LANES), F32),
                   jax.ShapeDtypeStruct((LANES, t), F32), jax.ShapeDtypeStruct((SUBLANES, LANES), F32)],
        scratch_shapes=[pltpu.VMEM((SUBLANES, LANES), F32)],
        compiler_params=pltpu.CompilerParams(
            dimension_semantics=("arbitrary",), vmem_limit_bytes=VMEM_LIMIT),
        name="router",
    )(x2, *consts)


GATHER_UNROLL = 16
DMA_PRIORITIES = 2


def _token_tile(ref, r):
    return ref.at[pl.ds(pl.multiple_of(r * D_TILES, D_TILES), D_TILES), :]


def _start_row_gather(idx_ref, src_hbm, dst, sem, priorities=1):
    def body(g, c):
        for u in range(GATHER_UNROLL):
            r = g * GATHER_UNROLL + u
            pltpu.make_async_copy(_token_tile(src_hbm, idx_ref[0, 0, r]), _token_tile(dst, r), sem).start(
                priority=u % priorities)
        return c

    lax.fori_loop(0, dst.shape[0] // D_TILES // GATHER_UNROLL, body, 0)


def _wait_row_gather(src_hbm, dst, sem):
    pltpu.make_async_copy(src_hbm.at[pl.ds(0, dst.shape[0]), :], dst, sem).wait()


def _group_kernel(texp_ref, tvalid_ref, rows_ref, rows_next_ref, h_hbm, w1_ref, w3_ref, w2_ref, ys_ref,
                  xbuf, sem):
    i = pl.program_id(0)
    slot = i % 2

    @pl.when(i == 0)
    def _():
        _start_row_gather(rows_ref, h_hbm, xbuf.at[0], sem.at[0], DMA_PRIORITIES)

    @pl.when(i + 1 < pl.num_programs(0))
    def _():
        _start_row_gather(rows_next_ref, h_hbm, xbuf.at[1 - slot], sem.at[1 - slot], DMA_PRIORITIES)

    _wait_row_gather(h_hbm, xbuf.at[slot], sem.at[slot])

    @pl.when(tvalid_ref[i] == 1)
    def _():
        xb = _from_token_tiles(xbuf, GROUP_TILE, (slot,)).astype(BF16)
        _to_token_tiles(ys_ref, _swiglu_rows(xb, w1_ref, w3_ref, w2_ref, (0,)))

    @pl.when(tvalid_ref[i] == 0)
    def _():
        ys_ref[...] = jnp.zeros_like(ys_ref)


def _group_call(tile_expert, tile_valid, row_token, h3, w1, w3, w2):
    n_tiles = tile_expert.shape[0]
    tm = GROUP_TILE
    rows = row_token.reshape(n_tiles, 1, tm)

    def wspec(arr):
        return pl.BlockSpec((1,) + arr.shape[1:], lambda i, te, tv: (te[i], 0, 0))

    grid_spec = pltpu.PrefetchScalarGridSpec(
        num_scalar_prefetch=2,
        grid=(n_tiles,),
        in_specs=[
            pl.BlockSpec((1, 1, tm), lambda i, te, tv: (i, 0, 0), memory_space=pltpu.SMEM),
            pl.BlockSpec((1, 1, tm), lambda i, te, tv: (jnp.minimum(i + 1, n_tiles - 1), 0, 0),
                         memory_space=pltpu.SMEM),
            pl.BlockSpec(memory_space=pl.ANY),
            wspec(w1), wspec(w3), wspec(w2),
        ],
        out_specs=pl.BlockSpec((tm * D_TILES, LANES), lambda i, te, tv: (i, 0)),
        scratch_shapes=[pltpu.VMEM((2, tm * D_TILES, LANES), F32), pltpu.SemaphoreType.DMA((2,))],
    )
    return pl.pallas_call(
        _group_kernel,
        grid_spec=grid_spec,
        out_shape=jax.ShapeDtypeStruct((n_tiles * tm * D_TILES, LANES), F32),
        compiler_params=pltpu.CompilerParams(
            dimension_semantics=("arbitrary",), vmem_limit_bytes=VMEM_LIMIT),
        name="moe_group",
    )(tile_expert, tile_valid, rows, rows, h3, w1, w3, w2)


def _combine_kernel(p0_ref, p1_ref, p0_next_ref, p1_next_ref, ys_hbm, x_ref, route_ref, g_ref, o_ref,
                    buf, sem, *, apply_final_norm):
    i = pl.program_id(0)
    slot = i % 2

    def start(refs, s):
        for j in range(TOP_K):
            _start_row_gather(refs[j], ys_hbm, buf.at[s, j], sem.at[s, j], DMA_PRIORITIES)

    @pl.when(i == 0)
    def _():
        start((p0_ref, p1_ref), 0)

    @pl.when(i + 1 < pl.num_programs(0))
    def _():
        start((p0_next_ref, p1_next_ref), 1 - slot)

    for j in range(TOP_K):
        _wait_row_gather(ys_hbm, buf.at[slot, j], sem.at[slot, j])
    route = route_ref[...]
    y = x_ref[...]
    for j in range(TOP_K):
        y = y + route[:, ROUTE_W0 + j:ROUTE_W0 + j + 1] * _from_token_tiles(buf, ROW_TILE, (slot, j))
    o_ref[...] = _rms(y, g_ref[...]) if apply_final_norm else y


def _combine_call(pos0, pos1, ys, x2, route, g, apply_final_norm):
    t, d = x2.shape
    tm = ROW_TILE
    n = t // tm

    def ispec(shift):
        return pl.BlockSpec((1, 1, tm), lambda i: (jnp.minimum(i + shift, n - 1), 0, 0),
                            memory_space=pltpu.SMEM)

    p0 = pos0.reshape(n, 1, tm)
    p1 = pos1.reshape(n, 1, tm)
    return pl.pallas_call(
        functools.partial(_combine_kernel, apply_final_norm=apply_final_norm),
        grid=(n,),
        in_specs=[ispec(0), ispec(0), ispec(1), ispec(1), pl.BlockSpec(memory_space=pl.ANY),
                  _row_spec(tm, d), _row_spec(tm, LANES), _const_spec(g)],
        out_specs=_row_spec(tm, d),
        out_shape=jax.ShapeDtypeStruct((t, d), F32),
        scratch_shapes=[pltpu.VMEM((2, TOP_K, tm * D_TILES, LANES), F32),
                        pltpu.SemaphoreType.DMA((2, TOP_K))],
        compiler_params=pltpu.CompilerParams(
            dimension_semantics=("arbitrary",), vmem_limit_bytes=VMEM_LIMIT),
        name="moe_combine",
    )(p0, p1, p0, p1, ys, x2, route, g)


def _final_norm_kernel(x_ref, g_ref, o_ref):
    o_ref[...] = _rms(x_ref[...], g_ref[...])


def _final_norm_call(x2, g):
    t, d = x2.shape
    tm = ROW_TILE
    return pl.pallas_call(
        _final_norm_kernel,
        grid=(t // tm,),
        in_specs=[_row_spec(tm, d), _const_spec(g)],
        out_specs=_row_spec(tm, d),
        out_shape=jax.ShapeDtypeStruct((t, d), F32),
        name="final_norm",
    )(x2, g)


def _routing_plan(route_t, choice_counts, n_tiles):
    t = route_t.shape[1]
    tm = GROUP_TILE
    choice = [route_t[ROUTE_IDX0 + j].astype(jnp.int32) for j in range(TOP_K)]
    rank = [route_t[ROUTE_RANK0 + j].astype(jnp.int32) for j in range(TOP_K)]
    per_choice = jnp.stack([choice_counts[0, j * CHOICE_LANES:j * CHOICE_LANES + N_EXPERTS]
                            for j in range(TOP_K)]).astype(jnp.int32)
    counts = jnp.sum(per_choice, axis=0)
    tiles_per = (counts + tm - 1) // tm
    tile_end = jnp.cumsum(tiles_per)
    tile_start = tile_end - tiles_per
    earlier = jnp.cumsum(per_choice, axis=0) - per_choice
    pos = [tile_start[choice[j]] * tm + earlier[j][choice[j]] + rank[j] for j in range(TOP_K)]
    experts = jnp.concatenate(choice)
    tile_ids = jnp.arange(n_tiles, dtype=jnp.int32)
    tile_expert = jnp.minimum(
        jnp.sum((tile_ids[:, None] >= tile_end[None, :]).astype(jnp.int32), axis=1), N_EXPERTS - 1)
    tile_valid = (tile_ids < tile_end[-1]).astype(jnp.int32)
    order = jnp.argsort(experts, stable=True).astype(jnp.int32)
    row_expert = jnp.repeat(tile_expert, tm)
    row_rank = jnp.arange(n_tiles * tm, dtype=jnp.int32) - tile_start[row_expert] * tm
    row_used = (row_rank < counts[row_expert]) & (jnp.repeat(tile_valid, tm) == 1)
    src = jnp.clip((jnp.cumsum(counts) - counts)[row_expert] + row_rank, 0, t * TOP_K - 1)
    row_token = jnp.where(row_used, order[src] % t, 0)
    last_expert = tile_expert[jnp.maximum(tile_end[-1] - 1, 0)]
    tile_expert = jnp.where(tile_valid == 1, tile_expert, last_expert)
    return tile_expert, tile_valid, row_token, pos[0], pos[1]


def _pack_mixer_weights(w_in):
    glr = jnp.pad(w_in[:, OFF_GLR:GATE_OFF], ((0, 0), (0, LANES - GLA_RANK)))
    return jnp.concatenate([w_in[:, :OFF_GLR], glr], axis=1).astype(BF16)


def _row(v):
    return v.reshape(1, -1).astype(F32)


def kernel(x, norm_mix, w_in, b_gate, sgu_w, sgu_b, sgu_norm, conv_w, conv_b, pool_w, pool_scale, gla_wg2, gla_bg, gla_norm, branch_proj, w_out, norm_ffn, ffn_w1, ffn_w3, ffn_w2, moe_router, moe_w1, moe_w3, moe_w2, final_norm):
    b, s, d = x.shape
    t = b * s
    depth = norm_mix.shape[0]
    causal = jnp.tril(jnp.ones((SGU_CHUNK, SGU_CHUNK), dtype=bool))
    out = None
    for l in range(depth):
        w_mix = _pack_mixer_weights(w_in[l])
        sguw = jnp.concatenate([jnp.where(causal, sgu_w[l, hh], 0.0) for hh in range(SGU_HEADS)],
                               axis=0).astype(BF16)
        sgub = jnp.repeat(sgu_b[l].T, SGU_HEAD_DIM, axis=1).astype(F32)
        pw = jax.scipy.linalg.block_diag(*[pool_w[l, gi] for gi in range(len(POOL_WINDOWS))]).astype(BF16)
        wg2 = jnp.pad(gla_wg2[l], ((0, LANES - GLA_RANK), (0, 0))).astype(BF16)
        gn = jnp.tile(gla_norm[l], GLA_HEADS)
        i = l // 2
        slabs = {}
        if l == 0:
            slabs["branch_proj"] = branch_proj.reshape(-1, d)
            slabs["w_out"] = w_out.reshape(-1, d)
        if l % 2 == 0:
            slabs["ffn_w1"] = ffn_w1[i]
            slabs["ffn_w3"] = ffn_w3[i]
            slabs["ffn_w2"] = ffn_w2[i]
            if l + 1 < depth:
                slabs["moe_w1"] = moe_w1[(l + 1) // 2].reshape(-1, D_FF)
                slabs["moe_w3"] = moe_w3[(l + 1) // 2].reshape(-1, D_FF)
        else:
            slabs["moe_w2"] = moe_w2[i].reshape(-1, d)
        y, *cast = _mixer_call(x, _row(norm_mix[l]), w_mix, sguw, sgub, _row(sgu_norm[l]),
                               conv_w[l].astype(F32), _row(conv_b[l]), pw, _row(pool_scale[l]),
                               wg2, _row(gla_bg[l]), _row(gn), tuple(slabs.values()))
        cast = dict(zip(slabs, cast))
        if l == 0:
            proj_b = cast["branch_proj"].reshape(depth, N_BRANCH * BRANCH_WIDTH, d)
            w_out_b = cast["w_out"].reshape(depth, d, d)
        if "moe_w1" in cast:
            moe_up = [cast[k].reshape(N_EXPERTS, d, D_FF) for k in ("moe_w1", "moe_w3")]
        if "moe_w2" in cast:
            moe_down = cast["moe_w2"].reshape(N_EXPERTS, D_FF, d)
        x2 = _merge_call(x.reshape(t, d), y.reshape(t, N_BRANCH * BRANCH_WIDTH), _row(norm_mix[l]),
                         w_in[l][:, GATE_OFF:].astype(BF16), _row(b_gate[l]), proj_b[l], w_out_b[l])
        if l % 2 == 0:
            x2 = _ffn_call(x2, _row(norm_ffn[l]), cast["ffn_w1"], cast["ffn_w3"], cast["ffn_w2"])
            if l == depth - 1:
                out = _final_norm_call(x2, _row(final_norm))
        else:
            rw = jnp.pad(moe_router[i], ((0, 0), (0, LANES - N_EXPERTS)))
            rhi = rw.astype(BF16)
            h2, route, route_t, choice_counts = _router_call(
                x2, _row(norm_ffn[l]), rhi, (rw - rhi.astype(F32)).astype(BF16))
            n_tiles = (t * TOP_K) // GROUP_TILE + N_EXPERTS
            tile_expert, tile_valid, row_token, pos0, pos1 = _routing_plan(route_t, choice_counts, n_tiles)
            ys = _group_call(tile_expert, tile_valid, row_token, h2, moe_up[0], moe_up[1], moe_down)
            x2 = _combine_call(pos0, pos1, ys, x2, route, _row(final_norm), l == depth - 1)
            if l == depth - 1:
                out = x2
        x = x2.reshape(b, s, d)
    return out.reshape(b, s, d)
```

```python
import functools

import numpy as np
import jax
import jax.numpy as jnp
from jax import lax
from jax.experimental import pallas as pl
from jax.experimental.pallas import tpu as pltpu

F32 = jnp.float32
BF16 = jnp.bfloat16

D_MODEL = 1024
N_BRANCH = 4
BRANCH_WIDTH = 256
SGU_HEADS = 4
SGU_HEAD_DIM = BRANCH_WIDTH // SGU_HEADS
SGU_CHUNK = 128
CONV_WIDTH = 3
POOL_WINDOWS = (2, 4, 8, 16)
POOL_GROUP_DIM = BRANCH_WIDTH // len(POOL_WINDOWS)
POOL_HISTORY = 16
POOL_PAD = 8
assert POOL_WINDOWS == (2, 4, 8, 16)
CONV_HISTORY = 8
GLA_HEADS = 4
GLA_DK = 32
GLA_DV = BRANCH_WIDTH // GLA_HEADS
GLA_HK = GLA_HEADS * GLA_DK
GLA_RANK = 16
GLA_TAU = 16.0
GLA_CHUNK = 64
D_FF = 2816
N_EXPERTS = 8
TOP_K = 2
EPS = 1e-6

A_COLS = 2 * BRANCH_WIDTH
B_COLS = 3 * BRANCH_WIDTH
C_COLS = BRANCH_WIDTH
QKVR_COLS = 2 * GLA_HK + 2 * BRANCH_WIDTH
LANES = 128
BF16_SUBLANES = 16
OFF_A = 0
OFF_B = OFF_A + A_COLS
OFF_C = OFF_B + B_COLS
OFF_D = OFF_C + C_COLS
OFF_GLR = OFF_D + QKVR_COLS
MIX_COLS = OFF_GLR + LANES
GATE_OFF = OFF_GLR + GLA_RANK

MIX_TS = 512
ROW_TILE = 512
DENSE_TILE = 1024
GROUP_TILE = 512
FF_CHUNK = 256
VMEM_LIMIT = 56 * 1024 * 1024
MERGE_CHUNK = 256


def _dot(a, b):
    return jnp.dot(a, b, preferred_element_type=F32)


def _dot_nt(a, b):
    return lax.dot_general(a, b, (((1,), (1,)), ((), ())), preferred_element_type=F32)


def _dot_tn(a, b):
    return lax.dot_general(a, b, (((0,), (0,)), ((), ())), preferred_element_type=F32)


def _split(x):
    hi = x.astype(BF16)
    lo = (x - hi.astype(F32)).astype(BF16)
    return hi, lo


def _rms(x, g):
    ms = jnp.mean(x * x, axis=-1, keepdims=True)
    return x * lax.rsqrt(ms + EPS) * g


def _sigmoid(x):
    return 1.0 / (1.0 + jnp.exp(-x))


def _silu(x):
    return x * _sigmoid(x)


def _gelu_tanh(x):
    c = np.float32(np.sqrt(2.0 / np.pi))
    return x * (0.5 * (1.0 + jnp.tanh(c * (x + 0.044715 * (x * x * x)))))


def _group_mean_matrix(n, group):
    r = lax.broadcasted_iota(jnp.int32, (n, n), 0) // group
    c = lax.broadcasted_iota(jnp.int32, (n, n), 1) // group
    return jnp.where(r == c, 1.0 / group, 0.0).astype(BF16)


def _group_rms(x, bd, g):
    ms = _dot((x * x).astype(BF16), bd)
    return x * lax.rsqrt(ms + EPS) * g


def _mixer_kernel(x_ref, g_ref, w_ref, sguw_ref, sgub_ref, sgun_ref, cw_ref, cb_ref, pw_ref, ps_ref,
                  wg2_ref, bg_ref, gn_ref, *rest, n_cast):
    cast_in = rest[:n_cast]
    y_ref = rest[n_cast]
    cast_out = rest[n_cast + 1:2 * n_cast + 1]
    state_ref, ybuf_ref, zbuf_ref = rest[2 * n_cast + 1:]
    for src, dst in zip(cast_in, cast_out):
        dst[...] = src[...].astype(BF16)
    si = pl.program_id(1)
    ts = x_ref.shape[1]
    x = x_ref[0]
    h = _rms(x, g_ref[...]).astype(BF16)

    @pl.when(si == 0)
    def _():
        state_ref[...] = jnp.zeros_like(state_ref)
        ybuf_ref[0:CONV_HISTORY, :] = jnp.zeros((CONV_HISTORY, BRANCH_WIDTH), F32)
        zbuf_ref[:, 0:POOL_PAD + POOL_HISTORY, :] = jnp.zeros((3, POOL_PAD + POOL_HISTORY, BRANCH_WIDTH), F32)

    bd64 = _group_mean_matrix(BRANCH_WIDTH, SGU_HEAD_DIM)
    lane = lax.broadcasted_iota(jnp.int32, (1, BRANCH_WIDTH), 1)

    z = _gelu_tanh(_dot(h, w_ref[:, OFF_A:OFF_A + A_COLS]))
    u = z[:, :BRANCH_WIDTH]
    vn = _group_rms(z[:, BRANCH_WIDTH:], bd64, sgun_ref[...])
    lane_head = lane // SGU_HEAD_DIM
    vnb = vn.astype(BF16)
    for c in range(ts // SGU_CHUNK):
        rows = slice(c * SGU_CHUNK, (c + 1) * SGU_CHUNK)
        stacked = _dot(sguw_ref[...], vnb[rows])
        mixed = stacked[(SGU_HEADS - 1) * SGU_CHUNK:]
        for hh in range(SGU_HEADS - 2, -1, -1):
            mixed = jnp.where(lane_head == hh, stacked[hh * SGU_CHUNK:(hh + 1) * SGU_CHUNK], mixed)
        y_ref[0, rows, 0:BRANCH_WIDTH] = (u[rows] * (mixed + sgub_ref[...])).astype(BF16)

    zb = _dot(h, w_ref[:, OFF_B:OFF_B + B_COLS])
    gate_b = zb[:, BRANCH_WIDTH:2 * BRANCH_WIDTH]
    yv = zb[:, 2 * BRANCH_WIDTH:] * zb[:, :BRANCH_WIDTH]
    ybuf_ref[CONV_HISTORY:CONV_HISTORY + ts, :] = yv
    conv = cb_ref[...] + yv * cw_ref[CONV_WIDTH - 1:CONV_WIDTH, :]
    for i in range(CONV_WIDTH - 1):
        back = CONV_WIDTH - 1 - i
        conv = conv + ybuf_ref[CONV_HISTORY - back:CONV_HISTORY - back + ts, :] * cw_ref[i:i + 1, :]
    ybuf_ref[0:CONV_HISTORY, :] = ybuf_ref[ts:ts + CONV_HISTORY, :]
    y_ref[0, :, BRANCH_WIDTH:2 * BRANCH_WIDTH] = (gate_b * conv).astype(BF16)

    zc = _dot(h, w_ref[:, OFF_C:OFF_C + C_COLS])
    ext = POOL_HISTORY + ts
    p0 = POOL_PAD
    zbuf_ref[0, p0 + POOL_HISTORY:p0 + ext, :] = zc
    s2 = zbuf_ref[0, p0:p0 + ext, :] + zbuf_ref[0, p0 - 1:p0 - 1 + ext, :]
    zbuf_ref[1, p0:p0 + ext, :] = s2
    s4 = s2 + zbuf_ref[1, p0 - 2:p0 - 2 + ext, :]
    zbuf_ref[2, p0:p0 + ext, :] = s4
    s8 = s4 + zbuf_ref[2, p0 - 4:p0 - 4 + ext, :]
    sums = {2: s2[POOL_HISTORY:], 4: s4[POOL_HISTORY:], 8: s8[POOL_HISTORY:],
            16: s8[POOL_HISTORY:] + s8[POOL_HISTORY - 8:POOL_HISTORY - 8 + ts]}
    zbuf_ref[0, p0:p0 + POOL_HISTORY, :] = zbuf_ref[0, p0 + ts:p0 + ext, :]
    lane_group = lane // POOL_GROUP_DIM
    wsum = sums[POOL_WINDOWS[-1]]
    win = jnp.full((1, BRANCH_WIDTH), POOL_WINDOWS[-1], jnp.int32)
    for gi in range(len(POOL_WINDOWS) - 2, -1, -1):
        wsum = jnp.where(lane_group == gi, sums[POOL_WINDOWS[gi]], wsum)
        win = jnp.where(lane_group == gi, POOL_WINDOWS[gi], win)
    tpos = si * ts + lax.broadcasted_iota(jnp.int32, (ts, 1), 0)
    count = jnp.minimum(tpos + 1, win).astype(F32)
    pooled = wsum / count - zc
    y_ref[0, :, 2 * BRANCH_WIDTH:3 * BRANCH_WIDTH] = (
        _dot(pooled.astype(BF16), pw_ref[...]) * ps_ref[...]).astype(BF16)

    zd = _dot(h, w_ref[:, OFF_D:OFF_D + QKVR_COLS + LANES])
    q = zd[:, 0:GLA_HK]
    k = zd[:, GLA_HK:2 * GLA_HK]
    v = zd[:, 2 * GLA_HK:2 * GLA_HK + BRANCH_WIDTH]
    r = zd[:, 2 * GLA_HK + BRANCH_WIDTH:QKVR_COLS]
    glr = zd[:, QKVR_COLS:]
    a = _dot(glr.astype(BF16), wg2_ref[...]) + bg_ref[...]
    log_g = (jnp.minimum(a, 0.0) - jnp.log(1.0 + jnp.exp(-jnp.abs(a)))) * (1.0 / GLA_TAU)
    blk = 4 * GLA_CHUNK
    rr = lax.broadcasted_iota(jnp.int32, (blk, blk), 0)
    cc = lax.broadcasted_iota(jnp.int32, (blk, blk), 1)
    same = (rr // GLA_CHUNK) == (cc // GLA_CHUNK)
    tri = jnp.where(same & (cc <= rr), 1.0, 0.0).astype(BF16)
    cums = []
    for b in range(ts // blk):
        hi, lo = _split(log_g[b * blk:(b + 1) * blk])
        cums.append(_dot(tri, hi) + _dot(tri, lo))
    cum = jnp.concatenate(cums, axis=0)
    by_chunk = cum.reshape(ts // GLA_CHUNK, GLA_CHUNK, GLA_HK)
    last = jnp.broadcast_to(by_chunk[:, GLA_CHUNK - 1:, :], by_chunk.shape).reshape(ts, GLA_HK)
    qd_all = (q * (GLA_DK ** -0.5) * jnp.exp(cum)).astype(BF16)
    ki_all = (k * jnp.exp(-cum)).astype(BF16)
    ke_all = (k * jnp.exp(last - cum)).astype(BF16)
    v_all = v.astype(BF16)
    decay_all = jnp.exp(last)

    nrow = GLA_HEADS * GLA_CHUNK
    row_head = lax.broadcasted_iota(jnp.int32, (nrow, 1), 0) // GLA_CHUNK
    mask_k = jnp.where(row_head == lax.broadcasted_iota(jnp.int32, (nrow, GLA_HK), 1) // GLA_DK,
                       1.0, 0.0).astype(BF16)
    mask_v = jnp.where(row_head == lax.broadcasted_iota(jnp.int32, (nrow, BRANCH_WIDTH), 1) // GLA_DV,
                       1.0, 0.0).astype(BF16)
    causal = (lax.broadcasted_iota(jnp.int32, (GLA_CHUNK, nrow), 1) % GLA_CHUNK
              <= lax.broadcasted_iota(jnp.int32, (GLA_CHUNK, nrow), 0))
    mask_s = (lax.broadcasted_iota(jnp.int32, (BRANCH_WIDTH, GLA_HK), 0) // GLA_DV
              == lax.broadcasted_iota(jnp.int32, (BRANCH_WIDTH, GLA_HK), 1) // GLA_DK)

    state = state_ref[...]
    outs = []
    for n in range(ts // GLA_CHUNK):
        rows = slice(n * GLA_CHUNK, (n + 1) * GLA_CHUNK)
        qd = qd_all[rows]
        vc = v_all[rows]
        kbd = jnp.concatenate([ki_all[rows]] * GLA_HEADS, axis=0) * mask_k
        vbd = jnp.concatenate([vc] * GLA_HEADS, axis=0) * mask_v
        scores = jnp.where(causal, _dot_nt(qd, kbd), 0.0)
        outs.append(_dot(scores.astype(BF16), vbd) + _dot_nt(qd, state.astype(BF16)))
        kv = _dot_tn(vc, ke_all[rows])
        state = state * decay_all[n * GLA_CHUNK:n * GLA_CHUNK + 1, :] + jnp.where(mask_s, kv, 0.0)
    state_ref[...] = state
    on = _group_rms(jnp.concatenate(outs, axis=0), bd64, gn_ref[...])
    y_ref[0, :, 3 * BRANCH_WIDTH:] = (_silu(r) * on).astype(BF16)


def _mixer_call(x, g, w_mix, sguw, sgub, sgun, cw, cb, pw, ps, wg2, bg, gn, cast_slabs=()):
    b, s, d = x.shape
    ts = MIX_TS
    n_si = s // ts
    steps = b * n_si

    def const(arr):
        return pl.BlockSpec(arr.shape, lambda bi, si: (0,) * arr.ndim)

    def slab(arr):
        rows = arr.shape[0]
        per = next(r for r in range(BF16_SUBLANES, rows + 1, BF16_SUBLANES)
                   if rows % r == 0 and r * steps >= rows)
        last = rows // per - 1
        return pl.BlockSpec((per, arr.shape[1]), lambda bi, si: (jnp.minimum(bi * n_si + si, last), 0))

    consts = (g, w_mix, sguw, sgub, sgun, cw, cb, pw, ps, wg2, bg, gn)
    return pl.pallas_call(
        functools.partial(_mixer_kernel, n_cast=len(cast_slabs)),
        grid=(b, n_si),
        in_specs=([pl.BlockSpec((1, ts, d), lambda bi, si: (bi, si, 0))] + [const(c) for c in consts]
                  + [slab(a) for a in cast_slabs]),
        out_specs=([pl.BlockSpec((1, ts, N_BRANCH * BRANCH_WIDTH), lambda bi, si: (bi, si, 0))]
                   + [slab(a) for a in cast_slabs]),
        out_shape=([jax.ShapeDtypeStruct((b, s, N_BRANCH * BRANCH_WIDTH), BF16)]
                   + [jax.ShapeDtypeStruct(a.shape, BF16) for a in cast_slabs]),
        scratch_shapes=[
            pltpu.VMEM((BRANCH_WIDTH, GLA_HK), F32),
            pltpu.VMEM((CONV_HISTORY + ts, BRANCH_WIDTH), F32),
            pltpu.VMEM((3, POOL_PAD + POOL_HISTORY + ts, BRANCH_WIDTH), F32),
        ],
        compiler_params=pltpu.CompilerParams(
            dimension_semantics=("arbitrary", "arbitrary"), vmem_limit_bytes=VMEM_LIMIT),
        name="mixer",
    )(x, *consts, *cast_slabs)


def _merge_kernel(x_ref, y_ref, g_ref, wg_ref, bgate_ref, p_ref, wo_ref, o_ref):
    x = x_ref[...]
    h = _rms(x, g_ref[...]).astype(BF16)
    out = x
    for c in range(D_MODEL // MERGE_CHUNK):
        merged = None
        for i in range(N_BRANCH):
            cols = slice(i * D_MODEL + c * MERGE_CHUNK, i * D_MODEL + (c + 1) * MERGE_CHUNK)
            gate = _sigmoid(_dot(h, wg_ref[:, cols]) + bgate_ref[:, cols])
            rows = slice(i * BRANCH_WIDTH, (i + 1) * BRANCH_WIDTH)
            term = gate * _dot(y_ref[:, rows], p_ref[rows, c * MERGE_CHUNK:(c + 1) * MERGE_CHUNK])
            merged = term if merged is None else merged + term
        out = out + _dot(merged.astype(BF16), wo_ref[c * MERGE_CHUNK:(c + 1) * MERGE_CHUNK, :])
    o_ref[...] = out


def _row_spec(tm, width):
    return pl.BlockSpec((tm, width), lambda i: (i, 0))


def _const_spec(arr):
    return pl.BlockSpec(arr.shape, lambda i: (0,) * arr.ndim, pipeline_mode=pl.Buffered(1))


def _merge_call(x2, y2, g, wg, bgate, p, wo):
    t, d = x2.shape
    tm = DENSE_TILE
    consts = (g, wg, bgate, p, wo)
    return pl.pallas_call(
        _merge_kernel,
        grid=(t // tm,),
        in_specs=[_row_spec(tm, d), _row_spec(tm, y2.shape[1])] + [_const_spec(c) for c in consts],
        out_specs=_row_spec(tm, d),
        out_shape=jax.ShapeDtypeStruct((t, d), F32),
        compiler_params=pltpu.CompilerParams(
            dimension_semantics=("arbitrary",), vmem_limit_bytes=VMEM_LIMIT),
        name="merge",
    )(x2, y2, *consts)


FF_CHUNKS = D_FF // FF_CHUNK


def _swiglu_rows(hb, w1_ref, w3_ref, w2_ref, lead):
    acc = None
    for f in range(FF_CHUNKS):
        cols = slice(f * FF_CHUNK, (f + 1) * FF_CHUNK)
        up = _dot(hb, w1_ref[lead + (slice(None), cols)])
        gt = _dot(hb, w3_ref[lead + (slice(None), cols)])
        part = _dot((_silu(up) * gt).astype(BF16), w2_ref[lead + (cols, slice(None))])
        acc = part if acc is None else acc + part
    return acc


def _ffn_kernel(x_ref, g_ref, w1_ref, w3_ref, w2_ref, o_ref):
    x = x_ref[...]
    hb = _rms(x, g_ref[...]).astype(BF16)
    o_ref[...] = x + _swiglu_rows(hb, w1_ref, w3_ref, w2_ref, ())


def _ffn_call(x2, g, w1, w3, w2):
    t, d = x2.shape
    tm = DENSE_TILE
    consts = (g, w1, w3, w2)
    return pl.pallas_call(
        _ffn_kernel,
        grid=(t // tm,),
        in_specs=[_row_spec(tm, d)] + [_const_spec(c) for c in consts],
        out_specs=_row_spec(tm, d),
        out_shape=jax.ShapeDtypeStruct((t, d), F32),
        compiler_params=pltpu.CompilerParams(
            dimension_semantics=("arbitrary",), vmem_limit_bytes=VMEM_LIMIT),
        name="ffn",
    )(x2, *consts)


ROUTE_IDX0 = N_EXPERTS
ROUTE_W0 = N_EXPERTS + TOP_K
ROUTE_RANK0 = N_EXPERTS + 2 * TOP_K
CHOICE_LANES = LANES // TOP_K
SUBLANES = 8
D_TILES = D_MODEL // LANES


def _to_token_tiles(ref, value, lead=()):
    rows = value.shape[0]
    for c in range(D_TILES):
        ref[lead + (pl.ds(c, rows, stride=D_TILES), slice(None))] = value[:, c * LANES:(c + 1) * LANES]


def _from_token_tiles(ref, rows, lead=()):
    return jnp.concatenate(
        [ref[lead + (pl.ds(c, rows, stride=D_TILES), slice(None))] for c in range(D_TILES)], axis=1)


def _router_kernel(x_ref, g_ref, rhi_ref, rlo_ref, before_ref, h_ref, route_ref, route_t_ref, counts_ref, run_ref):
    step = pl.program_id(0)

    @pl.when(step == 0)
    def _():
        run_ref[...] = jnp.zeros_like(run_ref)

    x = x_ref[...]
    h = _rms(x, g_ref[...])
    _to_token_tiles(h_ref, h)
    hi, lo = _split(h)
    logits = _dot(hi, rhi_ref[...]) + _dot(hi, rlo_ref[...]) + _dot(lo, rhi_ref[...])
    lane = lax.broadcasted_iota(jnp.int32, logits.shape, 1)
    neg = jnp.float32(-jnp.inf)
    logits = jnp.where(lane < N_EXPERTS, logits, neg)
    m1 = jnp.max(logits, axis=-1, keepdims=True)
    i1 = jnp.min(jnp.where(logits == m1, lane, LANES), axis=-1, keepdims=True)
    rest = jnp.where(lane == i1, neg, logits)
    m2 = jnp.max(rest, axis=-1, keepdims=True)
    i2 = jnp.min(jnp.where(rest == m2, lane, LANES), axis=-1, keepdims=True)
    e2 = jnp.exp(m2 - m1)
    w1 = 1.0 / (1.0 + e2)
    w2 = e2 / (1.0 + e2)
    out = jnp.where(lane == ROUTE_IDX0, i1.astype(F32), 0.0)
    out = jnp.where(lane == ROUTE_IDX0 + 1, i2.astype(F32), out)
    out = jnp.where(lane == ROUTE_W0, w1, out)
    out = jnp.where(lane == ROUTE_W0 + 1, w2, out)
    onehots = [jnp.where(lane == idx + j * CHOICE_LANES, 1.0, 0.0) for j, idx in enumerate((i1, i2))]
    both = onehots[0] + onehots[1]
    seen = _dot(before_ref[...], both.astype(BF16)) + run_ref[0:1, :]
    for j in range(TOP_K):
        rank = jnp.sum(onehots[j] * seen, axis=-1, keepdims=True)
        out = jnp.where(lane == ROUTE_RANK0 + j, rank, out)
    run_ref[0:1, :] = run_ref[0:1, :] + jnp.sum(both, axis=0, keepdims=True)
    route_ref[...] = out
    route_t_ref[...] = out.T
    counts_ref[...] = run_ref[...]


def _router_call(x2, g, rhi, rlo):
    t, d = x2.shape
    tm = ROW_TILE
    before = jnp.tril(jnp.ones((tm, tm), BF16), -1)
    consts = (g, rhi, rlo, before)
    return pl.pallas_call(
        _router_kernel,
        grid=(t // tm,),
        in_specs=[_row_spec(tm, d)] + [_const_spec(c) for c in consts],
        out_specs=[_row_spec(tm * D_TILES, LANES), _row_spec(tm, LANES),
                   pl.BlockSpec((LANES, tm), lambda i: (0, i)), pl.BlockSpec((SUBLANES, LANES), lambda i: (0, 0))],
        out_shape=[jax.ShapeDtypeStruct((t * D_TILES, LANES), F32), jax.ShapeDtypeStruct((t, LANES), F32),
                   jax.ShapeDtypeStruct((LANES, t), F32), jax.ShapeDtypeStruct((SUBLANES, LANES), F32)],
        scratch_shapes=[pltpu.VMEM((SUBLANES, LANES), F32)],
        compiler_params=pltpu.CompilerParams(
            dimension_semantics=("arbitrary",), vmem_limit_bytes=VMEM_LIMIT),
        name="router",
    )(x2, *consts)


GATHER_UNROLL = 16
DMA_PRIORITIES = 2


def _token_tile(ref, r):
    return ref.at[pl.ds(pl.multiple_of(r * D_TILES, D_TILES), D_TILES), :]


def _start_row_gather(idx_ref, src_hbm, dst, sem, priorities=1):
    def body(g, c):
        for u in range(GATHER_UNROLL):
            r = g * GATHER_UNROLL + u
            pltpu.make_async_copy(_token_tile(src_hbm, idx_ref[0, 0, r]), _token_tile(dst, r), sem).start(
                priority=u % priorities)
        return c

    lax.fori_loop(0, dst.shape[0] // D_TILES // GATHER_UNROLL, body, 0)


def _wait_row_gather(src_hbm, dst, sem):
    pltpu.make_async_copy(src_hbm.at[pl.ds(0, dst.shape[0]), :], dst, sem).wait()


def _group_kernel(texp_ref, tvalid_ref, rows_ref, rows_next_ref, h_hbm, w1_ref, w3_ref, w2_ref, ys_ref,
                  xbuf, sem):
    i = pl.program_id(0)
    slot = i % 2

    @pl.when(i == 0)
    def _():
        _start_row_gather(rows_ref, h_hbm, xbuf.at[0], sem.at[0], DMA_PRIORITIES)

    @pl.when(i + 1 < pl.num_programs(0))
    def _():
        _start_row_gather(rows_next_ref, h_hbm, xbuf.at[1 - slot], sem.at[1 - slot], DMA_PRIORITIES)

    _wait_row_gather(h_hbm, xbuf.at[slot], sem.at[slot])

    @pl.when(tvalid_ref[i] == 1)
    def _():
        xb = _from_token_tiles(xbuf, GROUP_TILE, (slot,)).astype(BF16)
        _to_token_tiles(ys_ref, _swiglu_rows(xb, w1_ref, w3_ref, w2_ref, (0,)))

    @pl.when(tvalid_ref[i] == 0)
    def _():
        ys_ref[...] = jnp.zeros_like(ys_ref)


def _group_call(tile_expert, tile_valid, row_token, h3, w1, w3, w2):
    n_tiles = tile_expert.shape[0]
    tm = GROUP_TILE
    rows = row_token.reshape(n_tiles, 1, tm)

    def wspec(arr):
        return pl.BlockSpec((1,) + arr.shape[1:], lambda i, te, tv: (te[i], 0, 0))

    grid_spec = pltpu.PrefetchScalarGridSpec(
        num_scalar_prefetch=2,
        grid=(n_tiles,),
        in_specs=[
            pl.BlockSpec((1, 1, tm), lambda i, te, tv: (i, 0, 0), memory_space=pltpu.SMEM),
            pl.BlockSpec((1, 1, tm), lambda i, te, tv: (jnp.minimum(i + 1, n_tiles - 1), 0, 0),
                         memory_space=pltpu.SMEM),
            pl.BlockSpec(memory_space=pl.ANY),
            wspec(w1), wspec(w3), wspec(w2),
        ],
        out_specs=pl.BlockSpec((tm * D_TILES, LANES), lambda i, te, tv: (i, 0)),
        scratch_shapes=[pltpu.VMEM((2, tm * D_TILES, LANES), F32), pltpu.SemaphoreType.DMA((2,))],
    )
    return pl.pallas_call(
        _group_kernel,
        grid_spec=grid_spec,
        out_shape=jax.ShapeDtypeStruct((n_tiles * tm * D_TILES, LANES), F32),
        compiler_params=pltpu.CompilerParams(
            dimension_semantics=("arbitrary",), vmem_limit_bytes=VMEM_LIMIT),
        name="moe_group",
    )(tile_expert, tile_valid, rows, rows, h3, w1, w3, w2)


def _combine_kernel(p0_ref, p1_ref, p0_next_ref, p1_next_ref, ys_hbm, x_ref, route_ref, g_ref, o_ref,
                    buf, sem, *, apply_final_norm):
    i = pl.program_id(0)
    slot = i % 2

    def start(refs, s):
        for j in range(TOP_K):
            _start_row_gather(refs[j], ys_hbm, buf.at[s, j], sem.at[s, j], DMA_PRIORITIES)

    @pl.when(i == 0)
    def _():
        start((p0_ref, p1_ref), 0)

    @pl.when(i + 1 < pl.num_programs(0))
    def _():
        start((p0_next_ref, p1_next_ref), 1 - slot)

    for j in range(TOP_K):
        _wait_row_gather(ys_hbm, buf.at[slot, j], sem.at[slot, j])
    route = route_ref[...]
    y = x_ref[...]
    for j in range(TOP_K):
        y = y + route[:, ROUTE_W0 + j:ROUTE_W0 + j + 1] * _from_token_tiles(buf, ROW_TILE, (slot, j))
    o_ref[...] = _rms(y, g_ref[...]) if apply_final_norm else y


def _combine_call(pos0, pos1, ys, x2, route, g, apply_final_norm):
    t, d = x2.shape
    tm = ROW_TILE
    n = t // tm

    def ispec(shift):
        return pl.BlockSpec((1, 1, tm), lambda i: (jnp.minimum(i + shift, n - 1), 0, 0),
                            memory_space=pltpu.SMEM)

    p0 = pos0.reshape(n, 1, tm)
    p1 = pos1.reshape(n, 1, tm)
    return pl.pallas_call(
        functools.partial(_combine_kernel, apply_final_norm=apply_final_norm),
        grid=(n,),
        in_specs=[ispec(0), ispec(0), ispec(1), ispec(1), pl.BlockSpec(memory_space=pl.ANY),
                  _row_spec(tm, d), _row_spec(tm, LANES), _const_spec(g)],
        out_specs=_row_spec(tm, d),
        out_shape=jax.ShapeDtypeStruct((t, d), F32),
        scratch_shapes=[pltpu.VMEM((2, TOP_K, tm * D_TILES, LANES), F32),
                        pltpu.SemaphoreType.DMA((2, TOP_K))],
        compiler_params=pltpu.CompilerParams(
            dimension_semantics=("arbitrary",), vmem_limit_bytes=VMEM_LIMIT),
        name="moe_combine",
    )(p0, p1, p0, p1, ys, x2, route, g)


def _final_norm_kernel(x_ref, g_ref, o_ref):
    o_ref[...] = _rms(x_ref[...], g_ref[...])


def _final_norm_call(x2, g):
    t, d = x2.shape
    tm = ROW_TILE
    return pl.pallas_call(
        _final_norm_kernel,
        grid=(t // tm,),
        in_specs=[_row_spec(tm, d), _const_spec(g)],
        out_specs=_row_spec(tm, d),
        out_shape=jax.ShapeDtypeStruct((t, d), F32),
        name="final_norm",
    )(x2, g)


def _routing_plan(route_t, choice_counts, n_tiles):
    t = route_t.shape[1]
    tm = GROUP_TILE
    choice = [route_t[ROUTE_IDX0 + j].astype(jnp.int32) for j in range(TOP_K)]
    rank = [route_t[ROUTE_RANK0 + j].astype(jnp.int32) for j in range(TOP_K)]
    per_choice = jnp.stack([choice_counts[0, j * CHOICE_LANES:j * CHOICE_LANES + N_EXPERTS]
                            for j in range(TOP_K)]).astype(jnp.int32)
    counts = jnp.sum(per_choice, axis=0)
    tiles_per = (counts + tm - 1) // tm
    tile_end = jnp.cumsum(tiles_per)
    tile_start = tile_end - tiles_per
    earlier = jnp.cumsum(per_choice, axis=0) - per_choice
    pos = [tile_start[choice[j]] * tm + earlier[j][choice[j]] + rank[j] for j in range(TOP_K)]
    experts = jnp.concatenate(choice)
    tile_ids = jnp.arange(n_tiles, dtype=jnp.int32)
    tile_expert = jnp.minimum(
        jnp.sum((tile_ids[:, None] >= tile_end[None, :]).astype(jnp.int32), axis=1), N_EXPERTS - 1)
    tile_valid = (tile_ids < tile_end[-1]).astype(jnp.int32)
    order = jnp.argsort(experts, stable=True).astype(jnp.int32)
    row_expert = jnp.repeat(tile_expert, tm)
    row_rank = jnp.arange(n_tiles * tm, dtype=jnp.int32) - tile_start[row_expert] * tm
    row_used = (row_rank < counts[row_expert]) & (jnp.repeat(tile_valid, tm) == 1)
    src = jnp.clip((jnp.cumsum(counts) - counts)[row_expert] + row_rank, 0, t * TOP_K - 1)
    row_token = jnp.where(row_used, order[src] % t, 0)
    last_expert = tile_expert[jnp.maximum(tile_end[-1] - 1, 0)]
    tile_expert = jnp.where(tile_valid == 1, tile_expert, last_expert)
    return tile_expert, tile_valid, row_token, pos[0], pos[1]


def _pack_mixer_weights(w_in):
    glr = jnp.pad(w_in[:, OFF_GLR:GATE_OFF], ((0, 0), (0, LANES - GLA_RANK)))
    return jnp.concatenate([w_in[:, :OFF_GLR], glr], axis=1).astype(BF16)


def _row(v):
    return v.reshape(1, -1).astype(F32)


def kernel(x, norm_mix, w_in, b_gate, sgu_w, sgu_b, sgu_norm, conv_w, conv_b, pool_w, pool_scale, gla_wg2, gla_bg, gla_norm, branch_proj, w_out, norm_ffn, ffn_w1, ffn_w3, ffn_w2, moe_router, moe_w1, moe_w3, moe_w2, final_norm):
    b, s, d = x.shape
    t = b * s
    depth = norm_mix.shape[0]
    causal = jnp.tril(jnp.ones((SGU_CHUNK, SGU_CHUNK), dtype=bool))
    out = None
    for l in range(depth):
        w_mix = _pack_mixer_weights(w_in[l])
        sguw = jnp.concatenate([jnp.where(causal, sgu_w[l, hh], 0.0) for hh in range(SGU_HEADS)],
                               axis=0).astype(BF16)
        sgub = jnp.repeat(sgu_b[l].T, SGU_HEAD_DIM, axis=1).astype(F32)
        pw = jax.scipy.linalg.block_diag(*[pool_w[l, gi] for gi in range(len(POOL_WINDOWS))]).astype(BF16)
        wg2 = jnp.pad(gla_wg2[l], ((0, LANES - GLA_RANK), (0, 0))).astype(BF16)
        gn = jnp.tile(gla_norm[l], GLA_HEADS)
        i = l // 2
        slabs = {}
        if l == 0:
            slabs["branch_proj"] = branch_proj.reshape(-1, d)
            slabs["w_out"] = w_out.reshape(-1, d)
        if l % 2 == 0:
            slabs["ffn_w1"] = ffn_w1[i]
            slabs["ffn_w3"] = ffn_w3[i]
            slabs["ffn_w2"] = ffn_w2[i]
            if l + 1 < depth:
                slabs["moe_w1"] = moe_w1[(l + 1) // 2].reshape(-1, D_FF)
                slabs["moe_w3"] = moe_w3[(l + 1) // 2].reshape(-1, D_FF)
        else:
            slabs["moe_w2"] = moe_w2[i].reshape(-1, d)
        y, *cast = _mixer_call(x, _row(norm_mix[l]), w_mix, sguw, sgub, _row(sgu_norm[l]),
                               conv_w[l].astype(F32), _row(conv_b[l]), pw, _row(pool_scale[l]),
                               wg2, _row(gla_bg[l]), _row(gn), tuple(slabs.values()))
        cast = dict(zip(slabs, cast))
        if l == 0:
            proj_b = cast["branch_proj"].reshape(depth, N_BRANCH * BRANCH_WIDTH, d)
            w_out_b = cast["w_out"].reshape(depth, d, d)
        if "moe_w1" in cast:
            moe_up = [cast[k].reshape(N_EXPERTS, d, D_FF) for k in ("moe_w1", "moe_w3")]
        if "moe_w2" in cast:
            moe_down = cast["moe_w2"].reshape(N_EXPERTS, D_FF, d)
        x2 = _merge_call(x.reshape(t, d), y.reshape(t, N_BRANCH * BRANCH_WIDTH), _row(norm_mix[l]),
                         w_in[l][:, GATE_OFF:].astype(BF16), _row(b_gate[l]), proj_b[l], w_out_b[l])
        if l % 2 == 0:
            x2 = _ffn_call(x2, _row(norm_ffn[l]), cast["ffn_w1"], cast["ffn_w3"], cast["ffn_w2"])
            if l == depth - 1:
                out = _final_norm_call(x2, _row(final_norm))
        else:
            rw = jnp.pad(moe_router[i], ((0, 0), (0, LANES - N_EXPERTS)))
            rhi = rw.astype(BF16)
            h2, route, route_t, choice_counts = _router_call(
                x2, _row(norm_ffn[l]), rhi, (rw - rhi.astype(F32)).astype(BF16))
            n_tiles = (t * TOP_K) // GROUP_TILE + N_EXPERTS
            tile_expert, tile_valid, row_token, pos0, pos1 = _routing_plan(route_t, choice_counts, n_tiles)
            ys = _group_call(tile_expert, tile_valid, row_token, h2, moe_up[0], moe_up[1], moe_down)
            x2 = _combine_call(pos0, pos1, ys, x2, route, _row(final_norm), l == depth - 1)
            if l == depth - 1:
                out = x2
        x = x2.reshape(b, s, d)
    return out.reshape(b, s, d)
```
